```python
import math
import jax
import jax.numpy as jnp
from jax import lax
import numpy as np

D_MODEL = 2048
BATCH = 2
SEQ = 16384
DEPTH = 2

HEAD_DIM = 128
A_HEADS = 8
A_KV_HEADS = 2
B_HEADS = 8
B_PATTERNS = ((128, 1), (512, 4), (2048, 16))
C_HEADS = 8
C_VALUE_DIM = 2 * HEAD_DIM
D_FF_DENSE = 5632
N_EXPERTS = 8
TOP_K = 2
D_FF_EXPERT = 7168
MOE_BLOCK = 2048
Q_BLOCK = 128
GRID_W = 64
ROPE_THETA = 10000.0
NORM_EPS = 1e-6
NEG_INF = -1e30
A_Q_DIM = A_HEADS * HEAD_DIM
A_KV_DIM = A_KV_HEADS * HEAD_DIM
B_DIM = B_HEADS * HEAD_DIM
L0_IN_DIM = A_Q_DIM + 2 * A_KV_DIM + 3 * B_DIM
L0_MIX_DIM = A_Q_DIM + B_DIM
C_QK_DIM = C_HEADS * 2 * HEAD_DIM
C_V_DIM = C_HEADS * C_VALUE_DIM
L1_IN_DIM = 2 * C_QK_DIM + C_V_DIM

kernel_name = "hybrid_gqa_dilated_diffattn_moe_encoder"


def rms_norm(x, g):
    xf = x.astype(jnp.float32)
    y = xf * lax.rsqrt(jnp.mean(xf * xf, axis=-1, keepdims=True) + NORM_EPS)
    return (y * g.astype(jnp.float32)).astype(x.dtype)


def alibi_slopes(n_heads):
    return 2.0 ** (-8.0 * jnp.arange(1, n_heads + 1, dtype=jnp.float32) / n_heads)


def axial_rope_angles(S):
    rows = S // GRID_W
    row = jnp.broadcast_to(jnp.arange(rows, dtype=jnp.float32)[:, None], (rows, GRID_W)).reshape(S)
    col = jnp.broadcast_to(jnp.arange(GRID_W, dtype=jnp.float32)[None, :], (rows, GRID_W)).reshape(S)
    n_freq = HEAD_DIM // 4
    inv = ROPE_THETA ** (-jnp.arange(n_freq, dtype=jnp.float32) / n_freq)
    return row[:, None] * inv, col[:, None] * inv


def rope_half(x, ang):
    c = jnp.cos(ang)[:, None, :]
    s = jnp.sin(ang)[:, None, :]
    x1, x2 = jnp.split(x, 2, axis=-1)
    return jnp.concatenate([x1 * c - x2 * s, x2 * c + x1 * s], axis=-1)


def apply_axial_rope(x, ang_row, ang_col):
    xf = x.astype(jnp.float32)
    half = HEAD_DIM // 2
    out = jnp.concatenate([rope_half(xf[..., :half], ang_row), rope_half(xf[..., half:], ang_col)], axis=-1)
    return out.astype(x.dtype)


def gqa_attention(q, k, v):
    B, S, Hq, Dh = q.shape
    Hkv = k.shape[2]
    G = Hq // Hkv
    nb = S // Q_BLOCK
    scale = Dh ** -0.5
    qb = q.reshape(B, nb, Q_BLOCK, Hkv, G, Dh).transpose(1, 0, 2, 3, 4, 5)

    def block(q_blk):
        s = jnp.einsum('bqhgd,bkhd->bhgqk', q_blk, k, preferred_element_type=jnp.float32) * scale
        p = jax.nn.softmax(s, axis=-1)
        return jnp.einsum('bhgqk,bkhd->bqhgd', p.astype(v.dtype), v)

    o = lax.map(block, qb)
    return o.transpose(1, 0, 2, 3, 4, 5).reshape(B, S, Hq, Dh)


def dilated_window_attention(q, k, v, slopes, window, dil):
    B, S, H, Dh = q.shape
    L = S // dil
    R = window // (2 * dil)
    W = R
    nb = -(-L // W)
    Lp = nb * W
    BD = B * dil
    scale = Dh ** -0.5

    def to_sub(t):
        return t.reshape(B, L, dil, H, Dh).transpose(0, 2, 1, 3, 4).reshape(BD, L, H, Dh)

    def key_blocks(t):
        tp = jnp.pad(to_sub(t), ((0, 0), (W, Lp - L + W), (0, 0), (0, 0))).reshape(BD, nb + 2, W, H, Dh)
        return jnp.concatenate([tp[:, :-2], tp[:, 1:-1], tp[:, 2:]], axis=2)

    qs = jnp.pad(to_sub(q), ((0, 0), (0, Lp - L), (0, 0), (0, 0))).reshape(BD, nb, W, H, Dh)
    kb = key_blocks(k)
    vb = key_blocks(v)
    rel = jnp.arange(3 * W)[None, :] - W - jnp.arange(W)[:, None]
    kidx = jnp.arange(nb)[:, None] * W - W + jnp.arange(3 * W)[None, :]
    valid = (jnp.abs(rel) <= R)[None] & ((kidx >= 0) & (kidx < L))[:, None, :]
    bias = -slopes[:, None, None] * (dil * jnp.abs(rel)).astype(jnp.float32)[None]
    s = jnp.einsum('znqhd,znkhd->znhqk', qs, kb, preferred_element_type=jnp.float32) * scale
    s = jnp.where(valid[None, :, None], s + bias[None, None], NEG_INF)
    m = jnp.max(s, axis=-1, keepdims=True)
    p = jnp.exp(s - m)
    l = jnp.sum(p, axis=-1, keepdims=True)
    o = jnp.einsum('znhqk,znkhd->znqhd', (p / l).astype(v.dtype), vb)
    lse = (m + jnp.log(l))[..., 0].transpose(0, 1, 3, 2)

    def from_sub(t):
        t = t[:, :L].reshape((B, dil, L) + t.shape[2:])
        t = jnp.moveaxis(t, 1, 2)
        return t.reshape((B, S) + t.shape[3:])

    return from_sub(o.reshape(BD, Lp, H, Dh)), from_sub(lse.reshape(BD, Lp, H))


def dilated_mixture_attention(q, k, v, slopes):
    outs = []
    lses = []
    for window, dil in B_PATTERNS:
        o_p, lse_p = dilated_window_attention(q, k, v, slopes, window, dil)
        outs.append(o_p)
        lses.append(lse_p)
    wts = jax.nn.softmax(jnp.stack(lses, axis=0), axis=0)
    o = jnp.einsum('pbsh,pbshd->bshd', wts, jnp.stack(outs, axis=0).astype(jnp.float32))
    return o.astype(q.dtype)


def diff_attention(q, k, v, lam, slopes):
    B, S, H, _, Dh = q.shape
    Dv = v.shape[-1]
    nb = S // Q_BLOCK
    scale = Dh ** -0.5
    kpos = jnp.arange(S, dtype=jnp.float32)
    qb = q.reshape(B, nb, Q_BLOCK, H, 2, Dh).transpose(1, 0, 2, 3, 4, 5)

    def block(args):
        q_blk, i = args
        qpos = (i * Q_BLOCK + jnp.arange(Q_BLOCK)).astype(jnp.float32)
        bias = -slopes[:, None, None] * jnp.abs(qpos[:, None] - kpos[None, :])[None]
        s = jnp.einsum('bqhcd,bkhcd->bhcqk', q_blk, k, preferred_element_type=jnp.float32) * scale
        p = jax.nn.softmax(s + bias[None, :, None], axis=-1)
        a = p[:, :, 0] - lam * p[:, :, 1]
        return jnp.einsum('bhqk,bkhd->bqhd', a.astype(v.dtype), v)

    o = lax.map(block, (qb, jnp.arange(nb)))
    return o.transpose(1, 0, 2, 3, 4).reshape(B, S, H, Dv)


def swiglu(h, w_gate, w_up, w_down):
    return (jax.nn.silu(h @ w_gate) * (h @ w_up)) @ w_down


def moe_swiglu(h, w_router, e_gate, e_up, e_down):
    T, D = h.shape
    logits = jnp.einsum('td,de->te', h, w_router, preferred_element_type=jnp.float32)
    top_val, top_idx = lax.top_k(logits, TOP_K)
    gates = jax.nn.softmax(top_val, axis=-1).astype(h.dtype)
    A = T * TOP_K
    a_exp = top_idx.reshape(A).astype(jnp.int32)
    a_tok = jnp.repeat(jnp.arange(T, dtype=jnp.int32), TOP_K)
    a_gate = gates.reshape(A)
    s_exp, s_tok, s_gate = lax.sort((a_exp, a_tok, a_gate), dimension=0, is_stable=True, num_keys=1)
    counts = jnp.bincount(a_exp, length=N_EXPERTS)
    padded = (counts + MOE_BLOCK - 1) // MOE_BLOCK * MOE_BLOCK
    start = jnp.cumsum(counts) - counts
    pend = jnp.cumsum(padded)
    pstart = pend - padded
    dest = pstart[s_exp] + jnp.arange(A, dtype=jnp.int32) - start[s_exp]
    n_blocks = -(-A // MOE_BLOCK) + N_EXPERTS
    P = n_blocks * MOE_BLOCK
    buf_tok = jnp.full((P,), T, jnp.int32).at[dest].set(s_tok)
    buf_gate = jnp.zeros((P,), h.dtype).at[dest].set(s_gate)
    blk_exp = jnp.minimum(jnp.searchsorted(pend, jnp.arange(n_blocks, dtype=jnp.int32) * MOE_BLOCK, side='right'), N_EXPERTS - 1)
    h_pad = jnp.concatenate([h, jnp.zeros((1, D), h.dtype)], axis=0)
    xs = h_pad[buf_tok].reshape(n_blocks, MOE_BLOCK, D)

    def expert_block(args):
        xb, e = args
        return (jax.nn.silu(xb @ e_gate[e]) * (xb @ e_up[e])) @ e_down[e]

    ys = lax.map(expert_block, (xs, blk_exp)).reshape(P, D)
    return jax.ops.segment_sum(ys * buf_gate[:, None], buf_tok, num_segments=T + 1)[:T]


def even_layer(x, norm_mix, w_in, qn_a, kn_a, qn_b, kn_b, w_out, norm_ffn, w_gate, w_up, w_down):
    B, S, _ = x.shape
    h = rms_norm(x, norm_mix)
    proj = jnp.einsum('bsd,de->bse', h, w_in)
    splits = np.cumsum([A_Q_DIM, A_KV_DIM, A_KV_DIM, B_DIM, B_DIM]).tolist()
    q_a, k_a, v_a, q_b, k_b, v_b = jnp.split(proj, splits, axis=-1)
    ang_row, ang_col = axial_rope_angles(S)
    q_a = apply_axial_rope(rms_norm(q_a.reshape(B, S, A_HEADS, HEAD_DIM), qn_a), ang_row, ang_col)
    k_a = apply_axial_rope(rms_norm(k_a.reshape(B, S, A_KV_HEADS, HEAD_DIM), kn_a), ang_row, ang_col)
    o_a = gqa_attention(q_a, k_a, v_a.reshape(B, S, A_KV_HEADS, HEAD_DIM))
    q_b = rms_norm(q_b.reshape(B, S, B_HEADS, HEAD_DIM), qn_b)
    k_b = rms_norm(k_b.reshape(B, S, B_HEADS, HEAD_DIM), kn_b)
    o_b = dilated_mixture_attention(q_b, k_b, v_b.reshape(B, S, B_HEADS, HEAD_DIM), alibi_slopes(B_HEADS))
    mix = jnp.concatenate([o_a.reshape(B, S, A_Q_DIM), o_b.reshape(B, S, B_DIM)], axis=-1)
    x = x + jnp.einsum('bse,ed->bsd', mix, w_out)
    return x + swiglu(rms_norm(x, norm_ffn), w_gate, w_up, w_down)


def odd_layer(x, layer, norm_mix, w_in, qn_c, kn_c, lam_q1, lam_k1, lam_q2, lam_k2, subln, w_out,
              norm_ffn, w_router, e_gate, e_up, e_down):
    B, S, D = x.shape
    h = rms_norm(x, norm_mix)
    proj = jnp.einsum('bsd,de->bse', h, w_in)
    q, k, v = jnp.split(proj, [C_QK_DIM, 2 * C_QK_DIM], axis=-1)
    q = rms_norm(q.reshape(B, S, C_HEADS, 2, HEAD_DIM), qn_c)
    k = rms_norm(k.reshape(B, S, C_HEADS, 2, HEAD_DIM), kn_c)
    v = v.reshape(B, S, C_HEADS, C_VALUE_DIM)
    lambda_init = 0.8 - 0.6 * math.exp(-0.3 * layer)
    f32 = jnp.float32
    lam = (jnp.exp(jnp.sum(lam_q1.astype(f32) * lam_k1.astype(f32)))
           - jnp.exp(jnp.sum(lam_q2.astype(f32) * lam_k2.astype(f32))) + lambda_init)
    o = diff_attention(q, k, v, lam, alibi_slopes(C_HEADS))
    o = rms_norm(o, subln) * (1.0 - lambda_init)
    x = x + jnp.einsum('bse,ed->bsd', o.reshape(B, S, C_V_DIM), w_out)
    hf = rms_norm(x, norm_ffn).reshape(B * S, D)
    return x + moe_swiglu(hf, w_router, e_gate, e_up, e_down).reshape(B, S, D)


def setup_inputs(seed: int = 0) -> dict:
    key = jax.random.key(seed)
    ks = iter(list(jax.random.split(key, 32)))

    def w(shape, fan_in):
        return jax.random.normal(next(ks), shape, jnp.float32) * (fan_in ** -0.5)

    def gain(n):
        return 1.0 + 0.02 * jax.random.normal(next(ks), (n,), jnp.float32)

    def small(shape, s):
        return s * jax.random.normal(next(ks), shape, jnp.float32)

    return {
        'x': jax.random.normal(next(ks), (BATCH, SEQ, D_MODEL), jnp.float32),
        'l0_norm_mix': gain(D_MODEL),
        'l0_w_in': w((D_MODEL, L0_IN_DIM), D_MODEL),
        'l0_qnorm_a': gain(HEAD_DIM),
        'l0_knorm_a': gain(HEAD_DIM),
        'l0_qnorm_b': gain(HEAD_DIM),
        'l0_knorm_b': gain(HEAD_DIM),
        'l0_w_out': w((L0_MIX_DIM, D_MODEL), L0_MIX_DIM),
        'l0_norm_ffn': gain(D_MODEL),
        'l0_w_gate': w((D_MODEL, D_FF_DENSE), D_MODEL),
        'l0_w_up': w((D_MODEL, D_FF_DENSE), D_MODEL),
        'l0_w_down': w((D_FF_DENSE, D_MODEL), D_FF_DENSE),
        'l1_norm_mix': gain(D_MODEL),
        'l1_w_in': w((D_MODEL, L1_IN_DIM), D_MODEL),
        'l1_qnorm_c': gain(HEAD_DIM),
        'l1_knorm_c': gain(HEAD_DIM),
        'l1_lambda_q1': small((HEAD_DIM,), 0.1),
        'l1_lambda_k1': small((HEAD_DIM,), 0.1),
        'l1_lambda_q2': small((HEAD_DIM,), 0.1),
        'l1_lambda_k2': small((HEAD_DIM,), 0.1),
        'l1_subln': gain(C_VALUE_DIM),
        'l1_w_out': w((C_V_DIM, D_MODEL), C_V_DIM),
        'l1_norm_ffn': gain(D_MODEL),
        'l1_w_router': w((D_MODEL, N_EXPERTS), D_MODEL),
        'l1_e_gate': w((N_EXPERTS, D_MODEL, D_FF_EXPERT), D_MODEL),
        'l1_e_up': w((N_EXPERTS, D_MODEL, D_FF_EXPERT), D_MODEL),
        'l1_e_down': w((N_EXPERTS, D_FF_EXPERT, D_MODEL), D_FF_EXPERT),
    }


def reference(x, l0_norm_mix, l0_w_in, l0_qnorm_a, l0_knorm_a, l0_qnorm_b, l0_knorm_b, l0_w_out,
              l0_norm_ffn, l0_w_gate, l0_w_up, l0_w_down,
              l1_norm_mix, l1_w_in, l1_qnorm_c, l1_knorm_c, l1_lambda_q1, l1_lambda_k1,
              l1_lambda_q2, l1_lambda_k2, l1_subln, l1_w_out, l1_norm_ffn, l1_w_router,
              l1_e_gate, l1_e_up, l1_e_down):
    for layer in range(DEPTH):
        if layer % 2 == 0:
            x = even_layer(x, l0_norm_mix, l0_w_in, l0_qnorm_a, l0_knorm_a, l0_qnorm_b, l0_knorm_b,
                           l0_w_out, l0_norm_ffn, l0_w_gate, l0_w_up, l0_w_down)
        else:
            x = odd_layer(x, layer, l1_norm_mix, l1_w_in, l1_qnorm_c, l1_knorm_c, l1_lambda_q1,
                          l1_lambda_k1, l1_lambda_q2, l1_lambda_k2, l1_subln, l1_w_out,
                          l1_norm_ffn, l1_w_router, l1_e_gate, l1_e_up, l1_e_down)
    return x
```

```python
import functools
import math

import jax
import jax.numpy as jnp
from jax import lax
from jax.experimental import pallas as pl
from jax.experimental.pallas import tpu as pltpu

F32 = jnp.float32
BF16 = jnp.bfloat16

HEAD_DIM = 128
LANES = 128
A_HEADS = 8
A_KV_HEADS = 2
B_HEADS = 8
B_PATTERNS = ((128, 1), (512, 4), (2048, 16))
C_HEADS = 8
N_EXPERTS = 8
TOP_K = 2
GRID_W = 64
ROPE_THETA = 10000.0
NORM_EPS = 1e-6
NEG_INF = -1e30
LOG2E = 1.4426950408889634
VMEM_LIMIT_BYTES = 56 * 1024 * 1024

_NT = (((1,), (1,)), ((), ()))


def _params(*sem):
    return pltpu.CompilerParams(dimension_semantics=sem, vmem_limit_bytes=VMEM_LIMIT_BYTES)


def _rms(x, eps=NORM_EPS):
    return x * lax.rsqrt(jnp.mean(x * x, axis=-1, keepdims=True) + eps)


def _normproj_kernel(*refs, kinds, heads_per_tile, has_rope):
    if has_rope:
        x_ref, g_ref, w_ref, hg_ref, cos_ref, sa_ref, sb_ref, o_ref, xn_ref = refs
    else:
        x_ref, g_ref, w_ref, hg_ref, o_ref, xn_ref = refs
    j = pl.program_id(1)

    @pl.when(j == 0)
    def _():
        xn_ref[...] = (_rms(x_ref[...]) * g_ref[...]).astype(BF16)

    acc = jnp.dot(xn_ref[...], w_ref[...], preferred_element_type=F32)

    for lo, hi, ops in kinds:
        @pl.when((j >= lo) & (j < hi))
        def _(ops=ops):
            for h in range(heads_per_tile):
                norm, rope, scale = ops[h]
                sl = slice(h * HEAD_DIM, (h + 1) * HEAD_DIM)
                y = acc[:, sl]
                if norm:
                    y = _rms(y) * hg_ref[:, sl]
                if rope:
                    y = (y * cos_ref[...] + pltpu.roll(y, 96, 1) * sa_ref[...]
                         + pltpu.roll(y, 32, 1) * sb_ref[...])
                if scale != 1.0:
                    y = y * scale
                o_ref[:, sl] = y.astype(o_ref.dtype)


def _normproj(x, g, w, head_gain, kinds, rope_tables, *, tm, tn):
    T, D = x.shape
    N = w.shape[1]
    has_rope = rope_tables is not None
    in_specs = [
        pl.BlockSpec((tm, D), lambda i, j: (i, 0)),
        pl.BlockSpec((1, D), lambda i, j: (0, 0)),
        pl.BlockSpec((D, tn), lambda i, j: (0, j)),
        pl.BlockSpec((1, tn), lambda i, j: (0, j)),
    ]
    args = [x, g.reshape(1, D), w, head_gain.reshape(1, N)]
    if has_rope:
        ns = rope_tables[0].shape[0] // tm
        for t in rope_tables:
            in_specs.append(pl.BlockSpec((tm, HEAD_DIM), lambda i, j: (i % ns, 0)))
            args.append(t)
    return pl.pallas_call(
        functools.partial(_normproj_kernel, kinds=kinds, heads_per_tile=tn // HEAD_DIM,
                          has_rope=has_rope),
        grid=(T // tm, N // tn),
        in_specs=in_specs,
        out_specs=pl.BlockSpec((tm, tn), lambda i, j: (i, j)),
        out_shape=jax.ShapeDtypeStruct((T, N), BF16),
        scratch_shapes=[pltpu.VMEM((tm, D), BF16)],
        compiler_params=_params("parallel", "arbitrary"),
        name="normproj",
    )(*args)


def _rope_tables(S):
    rows = S // GRID_W
    pos = jnp.arange(S, dtype=jnp.int32)
    row = (pos // GRID_W).astype(F32)
    col = (pos % GRID_W).astype(F32)
    del rows
    nf = HEAD_DIM // 4
    inv = ROPE_THETA ** (-jnp.arange(nf, dtype=F32) / nf)
    ang_row = row[:, None] * inv
    ang_col = col[:, None] * inv
    ang = jnp.concatenate([ang_row, ang_row, ang_col, ang_col], axis=-1)
    cos = jnp.cos(ang)
    sin = jnp.sin(ang)
    quarter = (jnp.arange(HEAD_DIM) // nf) % 2
    sa = jnp.where(quarter[None, :] == 0, -sin, 0.0)
    sb = jnp.where(quarter[None, :] == 1, sin, 0.0)
    return cos, sa, sb


def _gqa_kernel(q_ref, k_ref, v_ref, o_ref, m_ref, l_ref, acc_ref, *, group, nk):
    j = pl.program_id(3)

    @pl.when(j == 0)
    def _():
        m_ref[...] = jnp.full(m_ref.shape, NEG_INF, F32)
        l_ref[...] = jnp.zeros(l_ref.shape, F32)
        acc_ref[...] = jnp.zeros(acc_ref.shape, F32)

    k = k_ref[...]
    v = v_ref[...]
    for h in range(group):
        q = q_ref[:, h * HEAD_DIM:(h + 1) * HEAD_DIM]
        s = lax.dot_general(q, k, _NT, preferred_element_type=F32)
        m_prev = m_ref[h]
        m_new = jnp.maximum(m_prev, jnp.max(s, axis=1, keepdims=True))
        alpha = jnp.exp2(m_prev - m_new)
        p = jnp.exp2(s - m_new[:, :1])
        l_ref[h] = alpha * l_ref[h] + jnp.sum(p, axis=1, keepdims=True)
        acc_ref[h] = acc_ref[h] * alpha + jnp.dot(p.astype(BF16), v, preferred_element_type=F32)
        m_ref[h] = m_new

    @pl.when(j == nk - 1)
    def _():
        for h in range(group):
            o_ref[:, h * HEAD_DIM:(h + 1) * HEAD_DIM] = (acc_ref[h] / l_ref[h]).astype(o_ref.dtype)


def _gqa_attention(proj, B, S, *, q_col0, k_col0, v_col0, tq, tk):
    group = A_HEADS // A_KV_HEADS
    nq, nk = S // tq, S // tk
    gw = group * HEAD_DIM
    return pl.pallas_call(
        functools.partial(_gqa_kernel, group=group, nk=nk),
        grid=(B, A_KV_HEADS, nq, nk),
        in_specs=[
            pl.BlockSpec((tq, gw), lambda b, g, i, j: (b * nq + i, q_col0 // gw + g)),
            pl.BlockSpec((tk, HEAD_DIM), lambda b, g, i, j: (b * nk + j, k_col0 // HEAD_DIM + g)),
            pl.BlockSpec((tk, HEAD_DIM), lambda b, g, i, j: (b * nk + j, v_col0 // HEAD_DIM + g)),
        ],
        out_specs=pl.BlockSpec((tq, gw), lambda b, g, i, j: (b * nq + i, g)),
        out_shape=jax.ShapeDtypeStruct((B * S, A_HEADS * HEAD_DIM), BF16),
        scratch_shapes=[pltpu.VMEM((group, tq, LANES), F32),
                        pltpu.VMEM((group, tq, LANES), F32),
                        pltpu.VMEM((group, tq, HEAD_DIM), F32)],
        compiler_params=_params("parallel", "parallel", "parallel", "arbitrary"),
        name="gqa_attention",
    )(proj, proj, proj)


def _dilated_kernel(q_ref, *refs, nside, nq):
    nb = 2 * nside + 1
    k_refs, v_refs = refs[:nb], refs[nb:2 * nb]
    tbl_ref, o_ref = refs[2 * nb], refs[2 * nb + 1]
    i = pl.program_id(2)
    q = q_ref[...]
    scores = []
    m = None
    for d in range(nb):
        blk = i + (d - nside)
        in_range = (blk >= 0) & (blk < nq)
        s = lax.dot_general(q, k_refs[d][...], _NT, preferred_element_type=F32) + tbl_ref[0, d]
        s = jnp.where(in_range, s, NEG_INF)
        scores.append(s)
        md = jnp.max(s, axis=1, keepdims=True)
        m = md if m is None else jnp.maximum(m, md)
    l = None
    acc = None
    for d in range(nb):
        p = jnp.exp2(scores[d] - m)
        ld = jnp.sum(p, axis=1, keepdims=True)
        ad = jnp.dot(p.astype(BF16), v_refs[d][...], preferred_element_type=F32)
        l = ld if l is None else l + ld
        acc = ad if acc is None else acc + ad
    o_ref[...] = (acc / l).astype(o_ref.dtype)


def _dilated_bias_table(slopes, tq, nside):
    nb = 2 * nside + 1
    a = jnp.arange(tq, dtype=jnp.int32)
    d = ((jnp.arange(nb, dtype=jnp.int32)[:, None, None] - nside) * tq
         + a[None, None, :] - a[None, :, None])
    ad = jnp.abs(d)
    count = jnp.zeros(d.shape, F32)
    for window, dil in B_PATTERNS:
        reach = (window // (2 * dil)) * dil
        count = count + ((ad % dil == 0) & (ad <= reach)).astype(F32)
    logc = jnp.where(count > 0, jnp.log2(jnp.maximum(count, 1.0)), NEG_INF)
    bias = -(slopes * LOG2E)[:, None, None, None] * ad.astype(F32)[None]
    return jnp.where(count[None] > 0, bias + logc[None], NEG_INF)


def _dilated_attention(proj, B, S, slopes, *, q_col0, k_col0, v_col0, tq):
    reach = max((w // (2 * dl)) * dl for w, dl in B_PATTERNS)
    nside = -(-reach // tq)
    nb = 2 * nside + 1
    nq = S // tq
    table = _dilated_bias_table(slopes, tq, nside)

    def kv_spec(col0, d):
        def imap(h, b, i):
            return (b * nq + jnp.clip(i + (d - nside), 0, nq - 1), col0 // HEAD_DIM + h)
        return pl.BlockSpec((tq, HEAD_DIM), imap)

    in_specs = [pl.BlockSpec((tq, HEAD_DIM), lambda h, b, i: (b * nq + i, q_col0 // HEAD_DIM + h))]
    in_specs += [kv_spec(k_col0, d) for d in range(nb)]
    in_specs += [kv_spec(v_col0, d) for d in range(nb)]
    in_specs += [pl.BlockSpec((1, nb, tq, tq), lambda h, b, i: (h, 0, 0, 0))]
    return pl.pallas_call(
        functools.partial(_dilated_kernel, nside=nside, nq=nq),
        grid=(B_HEADS, B, nq),
        in_specs=in_specs,
        out_specs=pl.BlockSpec((tq, HEAD_DIM), lambda h, b, i: (b * nq + i, h)),
        out_shape=jax.ShapeDtypeStruct((B * S, B_HEADS * HEAD_DIM), BF16),
        compiler_params=_params("parallel", "parallel", "parallel"),
        name="dilated_attention",
    )(*([proj] * (1 + 2 * nb)), table)


def _outproj_kernel(*refs, n_parts):
    a_refs = refs[:n_parts]
    w_refs = refs[n_parts:2 * n_parts]
    r_ref, o_ref = refs[2 * n_parts], refs[2 * n_parts + 1]
    acc = r_ref[...]
    for a_ref, w_ref in zip(a_refs, w_refs):
        acc = acc + jnp.dot(a_ref[...], w_ref[...], preferred_element_type=F32)
    o_ref[...] = acc


def _outproj_residual(parts, w, res, *, tm, tn):
    T, N = res.shape
    n_parts = len(parts)
    in_specs, w_args, off = [], [], 0
    for a in parts:
        kp = a.shape[1]
        in_specs.append(pl.BlockSpec((tm, kp), lambda i, j: (i, 0)))
        w_args.append((kp, off // kp))
        off += kp
    for kp, blk in w_args:
        in_specs.append(pl.BlockSpec((kp, tn), lambda i, j, blk=blk: (blk, j)))
    in_specs.append(pl.BlockSpec((tm, tn), lambda i, j: (i, j)))
    return pl.pallas_call(
        functools.partial(_outproj_kernel, n_parts=n_parts),
        grid=(T // tm, N // tn),
        in_specs=in_specs,
        out_specs=pl.BlockSpec((tm, tn), lambda i, j: (i, j)),
        out_shape=jax.ShapeDtypeStruct((T, N), F32),
        compiler_params=_params("parallel", "arbitrary"),
        name="outproj_residual",
    )(*parts, *([w] * n_parts), res)


def _silu(g):
    return g / (1.0 + jnp.exp(-g))


def _ffn_kernel(x_ref, g_ref, wg_ref, wu_ref, wd_ref, o_ref, xn_ref):
    @pl.when(pl.program_id(1) == 0)
    def _():
        x = x_ref[...]
        xn_ref[...] = (_rms(x) * g_ref[...]).astype(BF16)
        o_ref[...] = x

    xn = xn_ref[...]
    gate = jnp.dot(xn, wg_ref[...], preferred_element_type=F32)
    up = jnp.dot(xn, wu_ref[...], preferred_element_type=F32)
    hid = (_silu(gate) * up).astype(BF16)
    o_ref[...] += jnp.dot(hid, wd_ref[...], preferred_element_type=F32)


def _ffn_residual(x, g, w_gate, w_up, w_down, *, tm, tf):
    T, D = x.shape
    Fd = w_gate.shape[1]
    nf = Fd // tf
    return pl.pallas_call(
        _ffn_kernel,
        grid=(T // tm, nf),
        in_specs=[
            pl.BlockSpec((tm, D), lambda i, f: (i, 0)),
            pl.BlockSpec((1, D), lambda i, f: (0, 0)),
            pl.BlockSpec((D, tf), lambda i, f: (0, f)),
            pl.BlockSpec((D, tf), lambda i, f: (0, f)),
            pl.BlockSpec((tf, D), lambda i, f: (f, 0)),
        ],
        out_specs=pl.BlockSpec((tm, D), lambda i, f: (i, 0)),
        out_shape=jax.ShapeDtypeStruct((T, D), F32),
        scratch_shapes=[pltpu.VMEM((tm, D), BF16)],
        compiler_params=_params("parallel", "arbitrary"),
        name="ffn_residual",
    )(x, g.reshape(1, D), w_gate, w_up, w_down)


def _diff_kernel(slopes_ref, q_ref, k_ref, v_ref, dmat_ref, lq1_ref, lk1_ref, lq2_ref, lk2_ref,
                 subln_ref, o_ref, m_ref, l_ref, acc_ref, *, tq, tk, nk, lambda_init):
    h = pl.program_id(1)
    i = pl.program_id(2)
    j = pl.program_id(3)

    @pl.when(j == 0)
    def _():
        m_ref[...] = jnp.full(m_ref.shape, NEG_INF, F32)
        l_ref[...] = jnp.zeros(l_ref.shape, F32)
        acc_ref[...] = jnp.zeros(acc_ref.shape, F32)

    neg_slope = -slopes_ref[h] * LOG2E
    offset = (j * tk - i * tq).astype(F32)
    bias = neg_slope * jnp.abs(dmat_ref[...] + offset)
    v = v_ref[...]
    for c in range(2):
        sl = slice(c * HEAD_DIM, (c + 1) * HEAD_DIM)
        s = lax.dot_general(q_ref[:, sl], k_ref[:, sl], _NT, preferred_element_type=F32) + bias
        m_prev = m_ref[c]
        m_new = jnp.maximum(m_prev, jnp.max(s, axis=1, keepdims=True))
        alpha = jnp.exp2(m_prev - m_new)
        p = jnp.exp2(s - m_new[:, :1])
        l_ref[c] = alpha * l_ref[c] + jnp.sum(p, axis=1, keepdims=True)
        acc_ref[c] = acc_ref[c] * alpha[:, :1] + jnp.dot(p.astype(BF16), v,
                                                         preferred_element_type=F32)
        m_ref[c] = m_new

    @pl.when(j == nk - 1)
    def _():
        lam = (jnp.exp(jnp.sum(lq1_ref[...] * lk1_ref[...], axis=-1, keepdims=True))
               - jnp.exp(jnp.sum(lq2_ref[...] * lk2_ref[...], axis=-1, keepdims=True))
               + lambda_init)
        o = acc_ref[0] / l_ref[0][:, :1] - lam * (acc_ref[1] / l_ref[1][:, :1])
        o = _rms(o) * subln_ref[...] * (1.0 - lambda_init)
        o_ref[...] = o.astype(o_ref.dtype)


def _diff_attention(proj, B, S, slopes, lam_vecs, subln, lambda_init, *, tq, tk):
    H = C_HEADS
    hw = 2 * HEAD_DIM
    nq, nk = S // tq, S // tk
    dmat = (jnp.arange(tk, dtype=F32)[None, :] - jnp.arange(tq, dtype=F32)[:, None])
    vec_spec = pl.BlockSpec((1, HEAD_DIM), lambda b, h, i, j, s: (0, 0))
    grid_spec = pltpu.PrefetchScalarGridSpec(
        num_scalar_prefetch=1,
        grid=(B, H, nq, nk),
        in_specs=[
            pl.BlockSpec((tq, hw), lambda b, h, i, j, s: (b * nq + i, h)),
            pl.BlockSpec((tk, hw), lambda b, h, i, j, s: (b * nk + j, H + h)),
            pl.BlockSpec((tk, hw), lambda b, h, i, j, s: (b * nk + j, 2 * H + h)),
            pl.BlockSpec((tq, tk), lambda b, h, i, j, s: (0, 0)),
            vec_spec, vec_spec, vec_spec, vec_spec,
            pl.BlockSpec((1, hw), lambda b, h, i, j, s: (0, 0)),
        ],
        out_specs=pl.BlockSpec((tq, hw), lambda b, h, i, j, s: (b * nq + i, h)),
        scratch_shapes=[pltpu.VMEM((2, tq, LANES), F32),
                        pltpu.VMEM((2, tq, LANES), F32),
                        pltpu.VMEM((2, tq, hw), F32)],
    )
    return pl.pallas_call(
        functools.partial(_diff_kernel, tq=tq, tk=tk, nk=nk, lambda_init=lambda_init),
        grid_spec=grid_spec,
        out_shape=jax.ShapeDtypeStruct((B * S, H * hw), BF16),
        compiler_params=_params("parallel", "parallel", "parallel", "arbitrary"),
        name="diff_attention",
    )(slopes, proj, proj, proj, dmat, *[v.reshape(1, HEAD_DIM) for v in lam_vecs],
      subln.reshape(1, hw))


def _router_kernel(x_ref, g_ref, wr_ref, meta_ref, cnt_ref, base_ref, *, tm, n_exp):
    i = pl.program_id(0)

    @pl.when(i == 0)
    def _():
        base_ref[...] = jnp.zeros(base_ref.shape, F32)

    hn = _rms(x_ref[...]) * g_ref[...]
    logits = jnp.dot(hn, wr_ref[...], preferred_element_type=F32,
                     precision=lax.Precision.HIGHEST)
    lane = lax.broadcasted_iota(jnp.int32, (tm, LANES), 1)
    logits = jnp.where(lane < n_exp, logits, NEG_INF)
    t1 = jnp.max(logits, axis=1, keepdims=True)
    i1 = jnp.min(jnp.where(logits == t1, lane, LANES), axis=1, keepdims=True)
    rest = jnp.where(lane == i1, NEG_INF, logits)
    t2 = jnp.max(rest, axis=1, keepdims=True)
    i2 = jnp.min(jnp.where(rest == t2, lane, LANES), axis=1, keepdims=True)
    ex = jnp.exp(t2 - t1)
    g1 = 1.0 / (1.0 + ex)
    g2 = ex * g1
    oh1 = lane == i1
    oh2 = lane == i2
    member = jnp.where(oh1 | oh2, 1.0, 0.0)
    row = lax.broadcasted_iota(jnp.int32, (tm, tm), 0)
    col = lax.broadcasted_iota(jnp.int32, (tm, tm), 1)
    strict_lower = jnp.where(col < row, 1.0, 0.0).astype(BF16)
    before = jnp.dot(strict_lower, member.astype(BF16), preferred_element_type=F32) + base_ref[...]
    r1 = jnp.sum(jnp.where(oh1, before, 0.0), axis=1, keepdims=True)
    r2 = jnp.sum(jnp.where(oh2, before, 0.0), axis=1, keepdims=True)
    base_ref[...] += jnp.sum(member, axis=0, keepdims=True)
    meta = jnp.where(lane == 0, i1.astype(F32), 0.0)
    meta = jnp.where(lane == 1, i2.astype(F32), meta)
    meta = jnp.where(lane == 2, g1, meta)
    meta = jnp.where(lane == 3, g2, meta)
    meta = jnp.where(lane == 4, r1, meta)
    meta = jnp.where(lane == 5, r2, meta)
    meta_ref[...] = meta
    cnt_ref[...] = base_ref[...]


def _router(x, g, w_router, *, tm):
    T, D = x.shape
    n_exp = w_router.shape[1]
    wr = jnp.zeros((D, LANES), F32).at[:, :n_exp].set(w_router)
    return pl.pallas_call(
        functools.partial(_router_kernel, tm=tm, n_exp=n_exp),
        grid=(T // tm,),
        in_specs=[
            pl.BlockSpec((tm, D), lambda i: (i, 0)),
            pl.BlockSpec((1, D), lambda i: (0, 0)),
            pl.BlockSpec((D, LANES), lambda i: (0, 0)),
        ],
        out_specs=[pl.BlockSpec((tm, LANES), lambda i: (i, 0)),
                   pl.BlockSpec((1, LANES), lambda i: (0, 0))],
        out_shape=[jax.ShapeDtypeStruct((T, LANES), F32),
                   jax.ShapeDtypeStruct((1, LANES), F32)],
        scratch_shapes=[pltpu.VMEM((1, LANES), F32)],
        compiler_params=_params("arbitrary"),
        name="moe_router",
    )(x, g.reshape(1, D), wr)


def _row_copy(src_ref, src_row, dst_ref, dst_row, sem):
    return pltpu.make_async_copy(src_ref.at[pl.ds(src_row, 1)], dst_ref.at[pl.ds(dst_row, 1)], sem)


def _dispatch_kernel(dest_hbm, x_hbm, zeros_hbm, xs_hbm, idx_ref, idx_sem, row_sem, *, tm):
    del zeros_hbm
    i = pl.program_id(0)
    n_idx = TOP_K * tm
    idx_copy = pltpu.make_async_copy(dest_hbm.at[pl.ds(i * n_idx, n_idx)], idx_ref, idx_sem)
    idx_copy.start()
    idx_copy.wait()

    def issue(t, carry):
        for k in range(TOP_K):
            _row_copy(x_hbm, i * tm + t, xs_hbm, idx_ref[TOP_K * t + k], row_sem).start()
        return carry

    lax.fori_loop(0, tm, issue, 0)

    def drain(t, carry):
        for k in range(TOP_K):
            _row_copy(x_hbm, 0, xs_hbm, 0, row_sem).wait()
        return carry

    lax.fori_loop(0, tm, drain, 0)


def _dispatch(dest, x, n_rows, *, tm):
    T, D = x.shape
    return pl.pallas_call(
        functools.partial(_dispatch_kernel, tm=tm),
        grid=(T // tm,),
        in_specs=[pl.BlockSpec(memory_space=pl.ANY)] * 3,
        out_specs=pl.BlockSpec(memory_space=pl.ANY),
        out_shape=jax.ShapeDtypeStruct((n_rows, D), F32),
        input_output_aliases={2: 0},
        scratch_shapes=[pltpu.SMEM((TOP_K * tm,), jnp.int32),
                        pltpu.SemaphoreType.DMA(()), pltpu.SemaphoreType.DMA(())],
        compiler_params=pltpu.CompilerParams(dimension_semantics=("arbitrary",),
                                             has_side_effects=True),
        name="moe_dispatch",
    )(dest, x, jnp.zeros((n_rows, D), F32))


def _expert_kernel(be_ref, bn_ref, x_ref, g_ref, wg_ref, wu_ref, wd_ref, y_ref, xn_ref, *, bm):
    del be_ref
    n_valid = bn_ref[pl.program_id(0)]

    @pl.when(pl.program_id(1) == 0)
    def _():
        rows = lax.broadcasted_iota(jnp.int32, (bm, 1), 0)
        x = jnp.where(rows < n_valid, x_ref[...], 0.0)
        xn_ref[...] = (_rms(x) * g_ref[...]).astype(BF16)
        y_ref[...] = jnp.zeros(y_ref.shape, F32)

    @pl.when(n_valid > 0)
    def _():
        xn = xn_ref[...]
        gate = jnp.dot(xn, wg_ref[...], preferred_element_type=F32)
        up = jnp.dot(xn, wu_ref[...], preferred_element_type=F32)
        hid = (_silu(gate) * up).astype(BF16)
        y_ref[...] += jnp.dot(hid, wd_ref[...], preferred_element_type=F32)


def _expert_ffn(xs, g, e_gate, e_up, e_down, blk_expert, blk_valid, *, bm, tf):
    P, D = xs.shape
    Fd = e_gate.shape[2]
    nf = Fd // tf
    nblk = P // bm

    def f_idx(b, f, bn):
        return jnp.where(bn[b] > 0, f, nf - 1)

    grid_spec = pltpu.PrefetchScalarGridSpec(
        num_scalar_prefetch=2,
        grid=(nblk, nf),
        in_specs=[
            pl.BlockSpec((bm, D), lambda b, f, be, bn: (b, 0)),
            pl.BlockSpec((1, D), lambda b, f, be, bn: (0, 0)),
            pl.BlockSpec((None, D, tf), lambda b, f, be, bn: (be[b], 0, f_idx(b, f, bn))),
            pl.BlockSpec((None, D, tf), lambda b, f, be, bn: (be[b], 0, f_idx(b, f, bn))),
            pl.BlockSpec((None, tf, D), lambda b, f, be, bn: (be[b], f_idx(b, f, bn), 0)),
        ],
        out_specs=pl.BlockSpec((bm, D), lambda b, f, be, bn: (b, 0)),
        scratch_shapes=[pltpu.VMEM((bm, D), BF16)],
    )
    return pl.pallas_call(
        functools.partial(_expert_kernel, bm=bm),
        grid_spec=grid_spec,
        out_shape=jax.ShapeDtypeStruct((P, D), F32),
        compiler_params=_params("arbitrary", "arbitrary"),
        name="moe_experts",
    )(blk_expert, blk_valid, xs, g.reshape(1, D), e_gate, e_up, e_down)


def _combine_kernel(dest_hbm, ys_hbm, x_ref, meta_ref, o_ref, idx_ref, ybuf_ref, idx_sem, row_sem,
                    *, tm):
    i = pl.program_id(0)
    n_idx = TOP_K * tm
    idx_copy = pltpu.make_async_copy(dest_hbm.at[pl.ds(i * n_idx, n_idx)], idx_ref, idx_sem)
    idx_copy.start()
    idx_copy.wait()

    def issue(t, carry):
        for k in range(TOP_K):
            _row_copy(ys_hbm, idx_ref[TOP_K * t + k], ybuf_ref.at[k], t, row_sem).start()
        return carry

    lax.fori_loop(0, tm, issue, 0)

    def drain(t, carry):
        for k in range(TOP_K):
            _row_copy(ys_hbm, 0, ybuf_ref.at[k], 0, row_sem).wait()
        return carry

    lax.fori_loop(0, tm, drain, 0)

    meta = meta_ref[...]
    out = x_ref[...]
    for k in range(TOP_K):
        out = out + meta[:, 2 + k:3 + k] * ybuf_ref[k]
    o_ref[...] = out


def _combine(dest, ys, x, meta, *, tm):
    T, D = x.shape
    return pl.pallas_call(
        functools.partial(_combine_kernel, tm=tm),
        grid=(T // tm,),
        in_specs=[pl.BlockSpec(memory_space=pl.ANY), pl.BlockSpec(memory_space=pl.ANY),
                  pl.BlockSpec((tm, D), lambda i: (i, 0)),
                  pl.BlockSpec((tm, LANES), lambda i: (i, 0))],
        out_specs=pl.BlockSpec((tm, D), lambda i: (i, 0)),
        out_shape=jax.ShapeDtypeStruct((T, D), F32),
        scratch_shapes=[pltpu.SMEM((TOP_K * tm,), jnp.int32),
                        pltpu.VMEM((TOP_K, tm, D), F32),
                        pltpu.SemaphoreType.DMA(()), pltpu.SemaphoreType.DMA(())],
        compiler_params=_params("arbitrary"),
        name="moe_combine",
    )(dest, ys, x, meta)


def _moe_residual(x, g, w_router, e_gate, e_up, e_down, *, tm_route, tm_move, bm, tf):
    T, D = x.shape
    meta, counts = _router(x, g, w_router, tm=tm_route)
    counts = counts[0, :N_EXPERTS].astype(jnp.int32)
    padded = (counts + bm - 1) // bm * bm
    pend = jnp.cumsum(padded)
    pstart = pend - padded
    expert = meta[:, 0:TOP_K].astype(jnp.int32)
    rank = meta[:, 4:4 + TOP_K].astype(jnp.int32)
    dest = (pstart[expert] + rank).reshape(T * TOP_K)
    n_rows = T * TOP_K + N_EXPERTS * bm
    nblk = n_rows // bm
    blk_row0 = jnp.arange(nblk, dtype=jnp.int32) * bm
    blk_expert = jnp.minimum(jnp.searchsorted(pend, blk_row0, side='right'),
                             N_EXPERTS - 1).astype(jnp.int32)
    blk_valid = jnp.clip(counts[blk_expert] - (blk_row0 - pstart[blk_expert]), 0, bm)
    blk_valid = jnp.where(blk_row0 < pend[-1], blk_valid, 0).astype(jnp.int32)
    last_used = jnp.maximum(pend[-1] // bm - 1, 0)
    blk_expert = jnp.where(blk_row0 < pend[-1], blk_expert, blk_expert[last_used])

    xs = _dispatch(dest, x, n_rows, tm=tm_move)
    ys = _expert_ffn(xs, g, e_gate, e_up, e_down, blk_expert, blk_valid, bm=bm, tf=tf)
    return _combine(dest, ys, x, meta, tm=tm_move)


def _alibi_slopes(n_heads):
    return 2.0 ** (-8.0 * jnp.arange(1, n_heads + 1, dtype=F32) / n_heads)


def _tile(n, want):
    t = min(n, want)
    while n % t:
        t //= 2
    return t


Q_SCALE = HEAD_DIM ** -0.5 * LOG2E
PROJ_TN = 512


def _head_kinds(head_ops):
    hpt = PROJ_TN // HEAD_DIM
    return tuple((t, t + 1, tuple(head_ops[t * hpt:(t + 1) * hpt]))
                 for t in range(len(head_ops) // hpt))


def _layer0(x, norm_mix, w_in, qn_a, kn_a, qn_b, kn_b, w_out, norm_ffn, w_gate, w_up, w_down):
    B, S, D = x.shape
    T = B * S
    xf = x.reshape(T, D)
    tm = _tile(S, 1024)
    a_q, a_kv, b_dim = A_HEADS * HEAD_DIM, A_KV_HEADS * HEAD_DIM, B_HEADS * HEAD_DIM
    head_gain = jnp.concatenate([
        jnp.tile(qn_a, A_HEADS), jnp.tile(kn_a, A_KV_HEADS), jnp.ones((a_kv,), F32),
        jnp.tile(qn_b, B_HEADS), jnp.tile(kn_b, B_HEADS), jnp.ones((b_dim,), F32)])
    seg = [(a_q, (True, True, Q_SCALE)), (a_kv, (True, True, 1.0)), (a_kv, (False, False, 1.0)),
           (b_dim, (True, False, Q_SCALE)), (b_dim, (True, False, 1.0)), (b_dim, (False, False, 1.0))]
    head_ops = [op for width, op in seg for _ in range(width // HEAD_DIM)]
    proj = _normproj(xf, norm_mix, w_in.astype(BF16), head_gain, _head_kinds(head_ops),
                     _rope_tables(S), tm=tm, tn=PROJ_TN)
    o_a = _gqa_attention(proj, B, S, q_col0=0, k_col0=a_q, v_col0=a_q + a_kv,
                         tq=_tile(S, 512), tk=_tile(S, 1024))
    b0 = a_q + 2 * a_kv
    o_b = _dilated_attention(proj, B, S, _alibi_slopes(B_HEADS), q_col0=b0, k_col0=b0 + b_dim,
                             v_col0=b0 + 2 * b_dim, tq=_tile(S, 512))
    x1 = _outproj_residual([o_a, o_b], w_out.astype(BF16), xf, tm=tm, tn=512)
    x2 = _ffn_residual(x1, norm_ffn, w_gate.astype(BF16), w_up.astype(BF16), w_down.astype(BF16),
                       tm=_tile(T, 512), tf=512)
    return x2.reshape(B, S, D)


def _layer1(x, norm_mix, w_in, qn_c, kn_c, lam_q1, lam_k1, lam_q2, lam_k2, subln, w_out, norm_ffn,
            w_router, e_gate, e_up, e_down):
    B, S, D = x.shape
    T = B * S
    xf = x.reshape(T, D)
    tm = _tile(S, 1024)
    c_qk = C_HEADS * 2 * HEAD_DIM
    head_gain = jnp.concatenate([jnp.tile(qn_c, 2 * C_HEADS), jnp.tile(kn_c, 2 * C_HEADS),
                                 jnp.ones((c_qk,), F32)])
    head_ops = ([(True, False, Q_SCALE)] * (2 * C_HEADS) + [(True, False, 1.0)] * (2 * C_HEADS)
                + [(False, False, 1.0)] * (2 * C_HEADS))
    proj = _normproj(xf, norm_mix, w_in.astype(BF16), head_gain, _head_kinds(head_ops), None,
                     tm=tm, tn=PROJ_TN)
    lambda_init = 0.8 - 0.6 * math.exp(-0.3 * 1)
    o_c = _diff_attention(proj, B, S, _alibi_slopes(C_HEADS), (lam_q1, lam_k1, lam_q2, lam_k2),
                          subln, lambda_init, tq=_tile(S, 512), tk=_tile(S, 1024))
    x3 = _outproj_residual([o_c], w_out.astype(BF16), xf, tm=tm, tn=512)
    out = _moe_residual(x3, norm_ffn, w_router, e_gate.astype(BF16), e_up.astype(BF16),
                        e_down.astype(BF16), tm_route=_tile(T, 512), tm_move=_tile(T, 512),
                        bm=_tile(T, 512), tf=512)
    return out.reshape(B, S, D)


def kernel(x, l0_norm_mix,l0_w_in, l0_qnorm_a, l0_knorm_a, l0_qnorm_b, l0_knorm_b, l0_w_out, l0_norm_ffn, l0_w_gate, l0_w_up, l0_w_down, l1_norm_mix, l1_w_in, l1_qnorm_c, l1_knorm_c, l1_lambda_q1, l1_lambda_k1, l1_lambda_q2, l1_lambda_k2, l1_subln, l1_w_out, l1_norm_ffn, l1_w_router, l1_e_gate, l1_e_up, l1_e_down):
    x = _layer0(x, l0_norm_mix, l0_w_in, l0_qnorm_a, l0_knorm_a, l0_qnorm_b, l0_knorm_b, l0_w_out,
                l0_norm_ffn, l0_w_gate, l0_w_up, l0_w_down)
    return _layer1(x, l1_norm_mix, l1_w_in, l1_qnorm_c, l1_knorm_c, l1_lambda_q1, l1_lambda_k1,
                   l1_lambda_q2, l1_lambda_k2, l1_subln, l1_w_out, l1_norm_ffn, l1_w_router,
                   l1_e_gate, l1_e_up, l1_e_down)
```

```python
import functools
import math

import jax
import jax.numpy as jnp
from jax import lax
from jax.experimental import pallas as pl
from jax.experimental.pallas import tpu as pltpu

F32 = jnp.float32
BF16 = jnp.bfloat16

HEAD_DIM = 128
LANES = 128
A_HEADS = 8
A_KV_HEADS = 2
B_HEADS = 8
B_PATTERNS = ((128, 1), (512, 4), (2048, 16))
C_HEADS = 8
N_EXPERTS = 8
TOP_K = 2
GRID_W = 64
ROPE_THETA = 10000.0
NORM_EPS = 1e-6
NEG_INF = -1e30
LOG2E = 1.4426950408889634
VMEM_LIMIT_BYTES = 56 * 1024 * 1024

_NT = (((1,), (1,)), ((), ()))


def _params(*sem):
    return pltpu.CompilerParams(dimension_semantics=sem, vmem_limit_bytes=VMEM_LIMIT_BYTES)


def _rms(x, eps=NORM_EPS):
    return x * lax.rsqrt(jnp.mean(x * x, axis=-1, keepdims=True) + eps)


def _normproj_kernel(*refs, kinds, heads_per_tile, has_rope):
    if has_rope:
        x_ref, g_ref, w_ref, hg_ref, cos_ref, sa_ref, sb_ref, o_ref, xn_ref = refs
    else:
        x_ref, g_ref, w_ref, hg_ref, o_ref, xn_ref = refs
    j = pl.program_id(1)

    @pl.when(j == 0)
    def _():
        xn_ref[...] = (_rms(x_ref[...]) * g_ref[...]).astype(BF16)

    acc = jnp.dot(xn_ref[...], w_ref[...], preferred_element_type=F32)

    for lo, hi, ops in kinds:
        @pl.when((j >= lo) & (j < hi))
        def _(ops=ops):
            for h in range(heads_per_tile):
                norm, rope, scale = ops[h]
                sl = slice(h * HEAD_DIM, (h + 1) * HEAD_DIM)
                y = acc[:, sl]
                if norm:
                    y = _rms(y) * hg_ref[:, sl]
                if rope:
                    y = (y * cos_ref[...] + pltpu.roll(y, 96, 1) * sa_ref[...]
                         + pltpu.roll(y, 32, 1) * sb_ref[...])
                if scale != 1.0:
                    y = y * scale
                o_ref[:, sl] = y.astype(o_ref.dtype)


def _normproj(x, g, w, head_gain, kinds, rope_tables, *, tm, tn):
    T, D = x.shape
    N = w.shape[1]
    has_rope = rope_tables is not None
    in_specs = [
        pl.BlockSpec((tm, D), lambda i, j: (i, 0)),
        pl.BlockSpec((1, D), lambda i, j: (0, 0)),
        pl.BlockSpec((D, tn), lambda i, j: (0, j)),
        pl.BlockSpec((1, tn), lambda i, j: (0, j)),
    ]
    args = [x, g.reshape(1, D), w, head_gain.reshape(1, N)]
    if has_rope:
        ns = rope_tables[0].shape[0] // tm
        for t in rope_tables:
            in_specs.append(pl.BlockSpec((tm, HEAD_DIM), lambda i, j: (i % ns, 0)))
            args.append(t)
    return pl.pallas_call(
        functools.partial(_normproj_kernel, kinds=kinds, heads_per_tile=tn // HEAD_DIM,
                          has_rope=has_rope),
        grid=(T // tm, N // tn),
        in_specs=in_specs,
        out_specs=pl.BlockSpec((tm, tn), lambda i, j: (i, j)),
        out_shape=jax.ShapeDtypeStruct((T, N), BF16),
        scratch_shapes=[pltpu.VMEM((tm, D), BF16)],
        compiler_params=_params("parallel", "arbitrary"),
        name="normproj",
    )(*args)


def _rope_tables(S):
    rows = S // GRID_W
    pos = jnp.arange(S, dtype=jnp.int32)
    row = (pos // GRID_W).astype(F32)
    col = (pos % GRID_W).astype(F32)
    del rows
    nf = HEAD_DIM // 4
    inv = ROPE_THETA ** (-jnp.arange(nf, dtype=F32) / nf)
    ang_row = row[:, None] * inv
    ang_col = col[:, None] * inv
    ang = jnp.concatenate([ang_row, ang_row, ang_col, ang_col], axis=-1)
    cos = jnp.cos(ang)
    sin = jnp.sin(ang)
    quarter = (jnp.arange(HEAD_DIM) // nf) % 2
    sa = jnp.where(quarter[None, :] == 0, -sin, 0.0)
    sb = jnp.where(quarter[None, :] == 1, sin, 0.0)
    return cos, sa, sb


def _gqa_kernel(q_ref, k_ref, v_ref, o_ref, m_ref, l_ref, acc_ref, *, group, nk):
    j = pl.program_id(3)

    @pl.when(j == 0)
    def _():
        m_ref[...] = jnp.full(m_ref.shape, NEG_INF, F32)
        l_ref[...] = jnp.zeros(l_ref.shape, F32)
        acc_ref[...] = jnp.zeros(acc_ref.shape, F32)

    k = k_ref[...]
    v = v_ref[...]
    for h in range(group):
        q = q_ref[:, h * HEAD_DIM:(h + 1) * HEAD_DIM]
        s = lax.dot_general(q, k, _NT, preferred_element_type=F32)
        m_prev = m_ref[h]
        m_new = jnp.maximum(m_prev, jnp.max(s, axis=1, keepdims=True))
        alpha = jnp.exp2(m_prev - m_new)
        p = jnp.exp2(s - m_new[:, :1])
        l_ref[h] = alpha * l_ref[h] + jnp.sum(p, axis=1, keepdims=True)
        acc_ref[h] = acc_ref[h] * alpha + jnp.dot(p.astype(BF16), v, preferred_element_type=F32)
        m_ref[h] = m_new

    @pl.when(j == nk - 1)
    def _():
        for h in range(group):
            o_ref[:, h * HEAD_DIM:(h + 1) * HEAD_DIM] = (acc_ref[h] / l_ref[h]).astype(o_ref.dtype)


def _gqa_bounded_kernel(q_ref, k_ref, v_ref, o_ref, l_ref, acc_ref, *, group, nk):
    j = pl.program_id(3)

    @pl.when(j == 0)
    def _():
        l_ref[...] = jnp.zeros(l_ref.shape, F32)
        acc_ref[...] = jnp.zeros(acc_ref.shape, F32)

    k = k_ref[...]
    v = v_ref[...]
    for h in range(group):
        q = q_ref[:, h * HEAD_DIM:(h + 1) * HEAD_DIM]
        p = jnp.exp2(lax.dot_general(q, k, _NT, preferred_element_type=F32))
        l_ref[h] += jnp.sum(p, axis=1, keepdims=True)
        acc_ref[h] += jnp.dot(p.astype(BF16), v, preferred_element_type=F32)

    @pl.when(j == nk - 1)
    def _():
        for h in range(group):
            o_ref[:, h * HEAD_DIM:(h + 1) * HEAD_DIM] = (acc_ref[h] / l_ref[h]).astype(o_ref.dtype)


def _gqa_attention(proj, B, S, *, q_col0, k_col0, v_col0, tq, tk, bounded):
    group = A_HEADS // A_KV_HEADS
    nq, nk = S // tq, S // tk
    gw = group * HEAD_DIM
    if bounded:
        body = functools.partial(_gqa_bounded_kernel, group=group, nk=nk)
        scratch = [pltpu.VMEM((group, tq, LANES), F32), pltpu.VMEM((group, tq, HEAD_DIM), F32)]
    else:
        body = functools.partial(_gqa_kernel, group=group, nk=nk)
        scratch = [pltpu.VMEM((group, tq, LANES), F32), pltpu.VMEM((group, tq, LANES), F32),
                   pltpu.VMEM((group, tq, HEAD_DIM), F32)]
    return pl.pallas_call(
        body,
        grid=(B, A_KV_HEADS, nq, nk),
        in_specs=[
            pl.BlockSpec((tq, gw), lambda b, g, i, j: (b * nq + i, q_col0 // gw + g)),
            pl.BlockSpec((tk, HEAD_DIM), lambda b, g, i, j: (b * nk + j, k_col0 // HEAD_DIM + g)),
            pl.BlockSpec((tk, HEAD_DIM), lambda b, g, i, j: (b * nk + j, v_col0 // HEAD_DIM + g)),
        ],
        out_specs=pl.BlockSpec((tq, gw), lambda b, g, i, j: (b * nq + i, g)),
        out_shape=jax.ShapeDtypeStruct((B * S, A_HEADS * HEAD_DIM), BF16),
        scratch_shapes=scratch,
        compiler_params=_params("parallel", "parallel", "parallel", "arbitrary"),
        name="gqa_bounded" if bounded else "gqa_attention",
    )(proj, proj, proj)


def _dilated_kernel(q_ref, *refs, nside, nq):
    nb = 2 * nside + 1
    k_refs, v_refs = refs[:nb], refs[nb:2 * nb]
    tbl_ref, o_ref = refs[2 * nb], refs[2 * nb + 1]
    i = pl.program_id(2)
    q = q_ref[...]
    scores = []
    m = None
    for d in range(nb):
        blk = i + (d - nside)
        in_range = (blk >= 0) & (blk < nq)
        s = lax.dot_general(q, k_refs[d][...], _NT, preferred_element_type=F32) + tbl_ref[0, d]
        s = jnp.where(in_range, s, NEG_INF)
        scores.append(s)
        md = jnp.max(s, axis=1, keepdims=True)
        m = md if m is None else jnp.maximum(m, md)
    l = None
    acc = None
    for d in range(nb):
        p = jnp.exp2(scores[d] - m)
        ld = jnp.sum(p, axis=1, keepdims=True)
        ad = jnp.dot(p.astype(BF16), v_refs[d][...], preferred_element_type=F32)
        l = ld if l is None else l + ld
        acc = ad if acc is None else acc + ad
    o_ref[...] = (acc / l).astype(o_ref.dtype)


def _dilated_bias_table(slopes, tq, nside):
    nb = 2 * nside + 1
    a = jnp.arange(tq, dtype=jnp.int32)
    d = ((jnp.arange(nb, dtype=jnp.int32)[:, None, None] - nside) * tq
         + a[None, None, :] - a[None, :, None])
    ad = jnp.abs(d)
    count = jnp.zeros(d.shape, F32)
    for window, dil in B_PATTERNS:
        reach = (window // (2 * dil)) * dil
        count = count + ((ad % dil == 0) & (ad <= reach)).astype(F32)
    logc = jnp.where(count > 0, jnp.log2(jnp.maximum(count, 1.0)), NEG_INF)
    bias = -(slopes * LOG2E)[:, None, None, None] * ad.astype(F32)[None]
    return jnp.where(count[None] > 0, bias + logc[None], NEG_INF)


def _dilated_attention(proj, B, S, slopes, *, q_col0, k_col0, v_col0, tq):
    reach = max((w // (2 * dl)) * dl for w, dl in B_PATTERNS)
    nside = -(-reach // tq)
    nb = 2 * nside + 1
    nq = S // tq
    table = _dilated_bias_table(slopes, tq, nside)

    def kv_spec(col0, d):
        def imap(h, b, i):
            return (b * nq + jnp.clip(i + (d - nside), 0, nq - 1), col0 // HEAD_DIM + h)
        return pl.BlockSpec((tq, HEAD_DIM), imap)

    in_specs = [pl.BlockSpec((tq, HEAD_DIM), lambda h, b, i: (b * nq + i, q_col0 // HEAD_DIM + h))]
    in_specs += [kv_spec(k_col0, d) for d in range(nb)]
    in_specs += [kv_spec(v_col0, d) for d in range(nb)]
    in_specs += [pl.BlockSpec((1, nb, tq, tq), lambda h, b, i: (h, 0, 0, 0))]
    return pl.pallas_call(
        functools.partial(_dilated_kernel, nside=nside, nq=nq),
        grid=(B_HEADS, B, nq),
        in_specs=in_specs,
        out_specs=pl.BlockSpec((tq, HEAD_DIM), lambda h, b, i: (b * nq + i, h)),
        out_shape=jax.ShapeDtypeStruct((B * S, B_HEADS * HEAD_DIM), BF16),
        compiler_params=_params("parallel", "parallel", "parallel"),
        name="dilated_attention",
    )(*([proj] * (1 + 2 * nb)), table)


def _outproj_kernel(*refs, n_parts):
    a_refs = refs[:n_parts]
    w_refs = refs[n_parts:2 * n_parts]
    r_ref, o_ref = refs[2 * n_parts], refs[2 * n_parts + 1]
    acc = r_ref[...]
    for a_ref, w_ref in zip(a_refs, w_refs):
        acc = acc + jnp.dot(a_ref[...], w_ref[...], preferred_element_type=F32)
    o_ref[...] = acc


def _outproj_residual(parts, w, res, *, tm, tn):
    T, N = res.shape
    n_parts = len(parts)
    in_specs, w_args, off = [], [], 0
    for a in parts:
        kp = a.shape[1]
        in_specs.append(pl.BlockSpec((tm, kp), lambda i, j: (i, 0)))
        w_args.append((kp, off // kp))
        off += kp
    for kp, blk in w_args:
        in_specs.append(pl.BlockSpec((kp, tn), lambda i, j, blk=blk: (blk, j)))
    in_specs.append(pl.BlockSpec((tm, tn), lambda i, j: (i, j)))
    return pl.pallas_call(
        functools.partial(_outproj_kernel, n_parts=n_parts),
        grid=(T // tm, N // tn),
        in_specs=in_specs,
        out_specs=pl.BlockSpec((tm, tn), lambda i, j: (i, j)),
        out_shape=jax.ShapeDtypeStruct((T, N), F32),
        compiler_params=_params("parallel", "arbitrary"),
        name="outproj_residual",
    )(*parts, *([w] * n_parts), res)


def _silu(g):
    return g / (1.0 + jnp.exp(-g))


def _ffn_kernel(x_ref, g_ref, wg_ref, wu_ref, wd_ref, o_ref, xn_ref):
    @pl.when(pl.program_id(1) == 0)
    def _():
        x = x_ref[...]
        xn_ref[...] = (_rms(x) * g_ref[...]).astype(BF16)
        o_ref[...] = x

    xn = xn_ref[...]
    gate = jnp.dot(xn, wg_ref[...], preferred_element_type=F32)
    up = jnp.dot(xn, wu_ref[...], preferred_element_type=F32)
    hid = (_silu(gate) * up).astype(BF16)
    o_ref[...] += jnp.dot(hid, wd_ref[...], preferred_element_type=F32)


def _ffn_residual(x, g, w_gate, w_up, w_down, *, tm, tf):
    T, D = x.shape
    Fd = w_gate.shape[1]
    nf = Fd // tf
    return pl.pallas_call(
        _ffn_kernel,
        grid=(T // tm, nf),
        in_specs=[
            pl.BlockSpec((tm, D), lambda i, f: (i, 0)),
            pl.BlockSpec((1, D), lambda i, f: (0, 0)),
            pl.BlockSpec((D, tf), lambda i, f: (0, f)),
            pl.BlockSpec((D, tf), lambda i, f: (0, f)),
            pl.BlockSpec((tf, D), lambda i, f: (f, 0)),
        ],
        out_specs=pl.BlockSpec((tm, D), lambda i, f: (i, 0)),
        out_shape=jax.ShapeDtypeStruct((T, D), F32),
        scratch_shapes=[pltpu.VMEM((tm, D), BF16)],
        compiler_params=_params("parallel", "arbitrary"),
        name="ffn_residual",
    )(x, g.reshape(1, D), w_gate, w_up, w_down)


def _diff_kernel(slopes_ref, q_ref, k_ref, v_ref, dmat_ref, lq1_ref, lk1_ref, lq2_ref, lk2_ref,
                 subln_ref, o_ref, m_ref, l_ref, acc_ref, *, tq, tk, nk, lambda_init):
    h = pl.program_id(1)
    i = pl.program_id(2)
    j = pl.program_id(3)

    @pl.when(j == 0)
    def _():
        m_ref[...] = jnp.full(m_ref.shape, NEG_INF, F32)
        l_ref[...] = jnp.zeros(l_ref.shape, F32)
        acc_ref[...] = jnp.zeros(acc_ref.shape, F32)

    neg_slope = -slopes_ref[h] * LOG2E
    bias = _alibi_bias(dmat_ref[...], (j * tk - i * tq).astype(F32), neg_slope, tk)
    v = v_ref[...]
    for c in range(2):
        sl = slice(c * HEAD_DIM, (c + 1) * HEAD_DIM)
        s = lax.dot_general(q_ref[:, sl], k_ref[:, sl], _NT, preferred_element_type=F32) + bias
        m_prev = m_ref[c]
        m_new = jnp.maximum(m_prev, jnp.max(s, axis=1, keepdims=True))
        alpha = jnp.exp2(m_prev - m_new)
        p = jnp.exp2(s - m_new[:, :1])
        l_ref[c] = alpha * l_ref[c] + jnp.sum(p, axis=1, keepdims=True)
        acc_ref[c] = acc_ref[c] * alpha[:, :1] + jnp.dot(p.astype(BF16), v,
                                                         preferred_element_type=F32)
        m_ref[c] = m_new

    @pl.when(j == nk - 1)
    def _():
        _diff_finalize(acc_ref, l_ref, lq1_ref, lk1_ref, lq2_ref, lk2_ref, subln_ref, o_ref,
                       lambda_init)


def _alibi_bias(dbase, offset, neg_slope, tk):
    return jnp.concatenate([neg_slope * jnp.abs(dbase + (offset + float(cb * LANES)))
                            for cb in range(tk // LANES)], axis=1)


def _diff_finalize(acc_ref, l_ref, lq1_ref, lk1_ref, lq2_ref, lk2_ref, subln_ref, o_ref, lambda_init):
    lam = (jnp.exp(jnp.sum(lq1_ref[...] * lk1_ref[...], axis=-1, keepdims=True))
           - jnp.exp(jnp.sum(lq2_ref[...] * lk2_ref[...], axis=-1, keepdims=True))
           + lambda_init)
    o = acc_ref[0] / l_ref[0][:, :1] - lam * (acc_ref[1] / l_ref[1][:, :1])
    o = _rms(o) * subln_ref[...] * (1.0 - lambda_init)
    o_ref[...] = o.astype(o_ref.dtype)


def _diff_bounded_kernel(slopes_ref, q_ref, k_ref, v_ref, dmat_ref, lq1_ref, lk1_ref, lq2_ref,
                         lk2_ref, subln_ref, o_ref, l_ref, acc_ref, *, tq, tk, nk, lambda_init):
    h = pl.program_id(1)
    i = pl.program_id(2)
    j = pl.program_id(3)

    @pl.when(j == 0)
    def _():
        l_ref[...] = jnp.zeros(l_ref.shape, F32)
        acc_ref[...] = jnp.zeros(acc_ref.shape, F32)

    neg_slope = -slopes_ref[h] * LOG2E
    offset = (j * tk - i * tq).astype(F32)
    v = v_ref[...]
    for c in range(2):
        sl = slice(c * HEAD_DIM, (c + 1) * HEAD_DIM)
        s = lax.dot_general(q_ref[:, sl], k_ref[:, sl], _NT, preferred_element_type=F32)
        p = jnp.exp2(s + _alibi_bias(dmat_ref[...], offset, neg_slope, tk))
        l_ref[c] += jnp.sum(p, axis=1, keepdims=True)
        acc_ref[c] += jnp.dot(p.astype(BF16), v, preferred_element_type=F32)

    @pl.when(j == nk - 1)
    def _():
        _diff_finalize(acc_ref, l_ref, lq1_ref, lk1_ref, lq2_ref, lk2_ref, subln_ref, o_ref,
                       lambda_init)


def _diff_attention(proj, B, S, slopes, lam_vecs, subln, lambda_init, *, tq, tk, bounded):
    H = C_HEADS
    hw = 2 * HEAD_DIM
    nq, nk = S // tq, S // tk
    dmat = (jnp.arange(LANES, dtype=F32)[None, :] - jnp.arange(tq, dtype=F32)[:, None])
    stat = [pltpu.VMEM((2, tq, LANES), F32)]
    if bounded:
        body, name = _diff_bounded_kernel, "diff_bounded"
    else:
        body, name, stat = _diff_kernel, "diff_attention", stat * 2
    vec_spec = pl.BlockSpec((1, HEAD_DIM), lambda b, h, i, j, s: (0, 0))
    grid_spec = pltpu.PrefetchScalarGridSpec(
        num_scalar_prefetch=1,
        grid=(B, H, nq, nk),
        in_specs=[
            pl.BlockSpec((tq, hw), lambda b, h, i, j, s: (b * nq + i, h)),
            pl.BlockSpec((tk, hw), lambda b, h, i, j, s: (b * nk + j, H + h)),
            pl.BlockSpec((tk, hw), lambda b, h, i, j, s: (b * nk + j, 2 * H + h)),
            pl.BlockSpec((tq, LANES), lambda b, h, i, j, s: (0, 0)),
            vec_spec, vec_spec, vec_spec, vec_spec,
            pl.BlockSpec((1, hw), lambda b, h, i, j, s: (0, 0)),
        ],
        out_specs=pl.BlockSpec((tq, hw), lambda b, h, i, j, s: (b * nq + i, h)),
        scratch_shapes=stat + [pltpu.VMEM((2, tq, hw), F32)],
    )
    return pl.pallas_call(
        functools.partial(body, tq=tq, tk=tk, nk=nk, lambda_init=lambda_init),
        grid_spec=grid_spec,
        out_shape=jax.ShapeDtypeStruct((B * S, H * hw), BF16),
        compiler_params=_params("parallel", "parallel", "parallel", "arbitrary"),
        name=name,
    )(slopes, proj, proj, proj, dmat, *[v.reshape(1, HEAD_DIM) for v in lam_vecs],
      subln.reshape(1, hw))


def _router_kernel(x_ref, g_ref, wr_ref, meta_ref, cnt_ref, base_ref, *, tm, n_exp):
    i = pl.program_id(0)

    @pl.when(i == 0)
    def _():
        base_ref[...] = jnp.zeros(base_ref.shape, F32)

    hn = _rms(x_ref[...]) * g_ref[...]
    logits = jnp.dot(hn, wr_ref[...], preferred_element_type=F32,
                     precision=lax.Precision.HIGHEST)
    lane = lax.broadcasted_iota(jnp.int32, (tm, LANES), 1)
    logits = jnp.where(lane < n_exp, logits, NEG_INF)
    t1 = jnp.max(logits, axis=1, keepdims=True)
    i1 = jnp.min(jnp.where(logits == t1, lane, LANES), axis=1, keepdims=True)
    rest = jnp.where(lane == i1, NEG_INF, logits)
    t2 = jnp.max(rest, axis=1, keepdims=True)
    i2 = jnp.min(jnp.where(rest == t2, lane, LANES), axis=1, keepdims=True)
    ex = jnp.exp(t2 - t1)
    g1 = 1.0 / (1.0 + ex)
    g2 = ex * g1
    oh1 = lane == i1
    oh2 = lane == i2
    member = jnp.where(oh1 | oh2, 1.0, 0.0)
    row = lax.broadcasted_iota(jnp.int32, (tm, tm), 0)
    col = lax.broadcasted_iota(jnp.int32, (tm, tm), 1)
    strict_lower = jnp.where(col < row, 1.0, 0.0).astype(BF16)
    before = jnp.dot(strict_lower, member.astype(BF16), preferred_element_type=F32) + base_ref[...]
    r1 = jnp.sum(jnp.where(oh1, before, 0.0), axis=1, keepdims=True)
    r2 = jnp.sum(jnp.where(oh2, before, 0.0), axis=1, keepdims=True)
    base_ref[...] += jnp.sum(member, axis=0, keepdims=True)
    meta = jnp.where(lane == 0, i1.astype(F32), 0.0)
    meta = jnp.where(lane == 1, i2.astype(F32), meta)
    meta = jnp.where(lane == 2, g1, meta)
    meta = jnp.where(lane == 3, g2, meta)
    meta = jnp.where(lane == 4, r1, meta)
    meta = jnp.where(lane == 5, r2, meta)
    meta_ref[...] = meta
    cnt_ref[...] = base_ref[...]


def _router(x, g, w_router, *, tm):
    T, D = x.shape
    n_exp = w_router.shape[1]
    wr = jnp.zeros((D, LANES), F32).at[:, :n_exp].set(w_router)
    return pl.pallas_call(
        functools.partial(_router_kernel, tm=tm, n_exp=n_exp),
        grid=(T // tm,),
        in_specs=[
            pl.BlockSpec((tm, D), lambda i: (i, 0)),
            pl.BlockSpec((1, D), lambda i: (0, 0)),
            pl.BlockSpec((D, LANES), lambda i: (0, 0)),
        ],
        out_specs=[pl.BlockSpec((tm, LANES), lambda i: (i, 0)),
                   pl.BlockSpec((1, LANES), lambda i: (0, 0))],
        out_shape=[jax.ShapeDtypeStruct((T, LANES), F32),
                   jax.ShapeDtypeStruct((1, LANES), F32)],
        scratch_shapes=[pltpu.VMEM((1, LANES), F32)],
        compiler_params=_params("arbitrary"),
        name="moe_router",
    )(x, g.reshape(1, D), wr)


def _row_copy(src_ref, src_row, dst_ref, dst_row, sem):
    return pltpu.make_async_copy(src_ref.at[pl.ds(src_row, 1)], dst_ref.at[pl.ds(dst_row, 1)], sem)


def _dispatch_kernel(dest_hbm, x_ref, zeros_hbm, xs_hbm, idx_ref, idx_sem, row_sem, *, tm):
    del zeros_hbm
    i = pl.program_id(0)
    n_idx = TOP_K * tm
    idx_copy = pltpu.make_async_copy(dest_hbm.at[pl.ds(i * n_idx, n_idx)], idx_ref, idx_sem)
    idx_copy.start()
    idx_copy.wait()

    def issue(t, carry):
        for k in range(TOP_K):
            _row_copy(x_ref, t, xs_hbm, idx_ref[TOP_K * t + k], row_sem).start()
        return carry

    lax.fori_loop(0, tm, issue, 0)

    def drain(t, carry):
        for k in range(TOP_K):
            _row_copy(x_ref, 0, xs_hbm, 0, row_sem).wait()
        return carry

    lax.fori_loop(0, tm, drain, 0)


def _dispatch(dest, x, n_rows, *, tm):
    T, D = x.shape
    return pl.pallas_call(
        functools.partial(_dispatch_kernel, tm=tm),
        grid=(T // tm,),
        in_specs=[pl.BlockSpec(memory_space=pl.ANY),
                  pl.BlockSpec((tm, D), lambda i: (i, 0)),
                  pl.BlockSpec(memory_space=pl.ANY)],
        out_specs=pl.BlockSpec(memory_space=pl.ANY),
        out_shape=jax.ShapeDtypeStruct((n_rows, D), F32),
        input_output_aliases={2: 0},
        scratch_shapes=[pltpu.SMEM((TOP_K * tm,), jnp.int32),
                        pltpu.SemaphoreType.DMA(()), pltpu.SemaphoreType.DMA(())],
        compiler_params=pltpu.CompilerParams(dimension_semantics=("arbitrary",),
                                             vmem_limit_bytes=VMEM_LIMIT_BYTES,
                                             has_side_effects=True),
        name="moe_dispatch",
    )(dest, x, jnp.zeros((n_rows, D), F32))


def _expert_kernel(be_ref, bn_ref, x_ref, g_ref, wg_ref, wu_ref, wd_ref, y_ref, xn_ref, *, bm):
    del be_ref
    n_valid = bn_ref[pl.program_id(0)]

    @pl.when(pl.program_id(1) == 0)
    def _():
        rows = lax.broadcasted_iota(jnp.int32, (bm, 1), 0)
        x = jnp.where(rows < n_valid, x_ref[...], 0.0)
        xn_ref[...] = (_rms(x) * g_ref[...]).astype(BF16)
        y_ref[...] = jnp.zeros(y_ref.shape, F32)

    @pl.when(n_valid > 0)
    def _():
        xn = xn_ref[...]
        gate = jnp.dot(xn, wg_ref[...], preferred_element_type=F32)
        up = jnp.dot(xn, wu_ref[...], preferred_element_type=F32)
        hid = (_silu(gate) * up).astype(BF16)
        y_ref[...] += jnp.dot(hid, wd_ref[...], preferred_element_type=F32)


def _expert_ffn(xs, g, e_gate, e_up, e_down, blk_expert, blk_valid, *, bm, tf):
    P, D = xs.shape
    Fd = e_gate.shape[2]
    nf = Fd // tf
    nblk = P // bm

    def f_idx(b, f, bn):
        return jnp.where(bn[b] > 0, f, nf - 1)

    grid_spec = pltpu.PrefetchScalarGridSpec(
        num_scalar_prefetch=2,
        grid=(nblk, nf),
        in_specs=[
            pl.BlockSpec((bm, D), lambda b, f, be, bn: (b, 0)),
            pl.BlockSpec((1, D), lambda b, f, be, bn: (0, 0)),
            pl.BlockSpec((None, D, tf), lambda b, f, be, bn: (be[b], 0, f_idx(b, f, bn))),
            pl.BlockSpec((None, D, tf), lambda b, f, be, bn: (be[b], 0, f_idx(b, f, bn))),
            pl.BlockSpec((None, tf, D), lambda b, f, be, bn: (be[b], f_idx(b, f, bn), 0)),
        ],
        out_specs=pl.BlockSpec((bm, D), lambda b, f, be, bn: (b, 0)),
        scratch_shapes=[pltpu.VMEM((bm, D), BF16)],
    )
    return pl.pallas_call(
        functools.partial(_expert_kernel, bm=bm),
        grid_spec=grid_spec,
        out_shape=jax.ShapeDtypeStruct((P, D), F32),
        compiler_params=_params("arbitrary", "arbitrary"),
        name="moe_experts",
    )(blk_expert, blk_valid, xs, g.reshape(1, D), e_gate, e_up, e_down)


def _combine_kernel(dest_hbm, ys_hbm, x_ref, meta_ref, o_ref, idx_ref, ybuf_ref, idx_sem, row_sem,
                    *, tm):
    i = pl.program_id(0)
    n_idx = TOP_K * tm
    idx_copy = pltpu.make_async_copy(dest_hbm.at[pl.ds(i * n_idx, n_idx)], idx_ref, idx_sem)
    idx_copy.start()
    idx_copy.wait()

    def issue(t, carry):
        for k in range(TOP_K):
            _row_copy(ys_hbm, idx_ref[TOP_K * t + k], ybuf_ref.at[k], t, row_sem).start()
        return carry

    lax.fori_loop(0, tm, issue, 0)

    def drain(t, carry):
        for k in range(TOP_K):
            _row_copy(ys_hbm, 0, ybuf_ref.at[k], 0, row_sem).wait()
        return carry

    lax.fori_loop(0, tm, drain, 0)

    meta = meta_ref[...]
    out = x_ref[...]
    for k in range(TOP_K):
        out = out + meta[:, 2 + k:3 + k] * ybuf_ref[k]
    o_ref[...] = out


def _combine(dest, ys, x, meta, *, tm):
    T, D = x.shape
    return pl.pallas_call(
        functools.partial(_combine_kernel, tm=tm),
        grid=(T // tm,),
        in_specs=[pl.BlockSpec(memory_space=pl.ANY), pl.BlockSpec(memory_space=pl.ANY),
                  pl.BlockSpec((tm, D), lambda i: (i, 0)),
                  pl.BlockSpec((tm, LANES), lambda i: (i, 0))],
        out_specs=pl.BlockSpec((tm, D), lambda i: (i, 0)),
        out_shape=jax.ShapeDtypeStruct((T, D), F32),
        scratch_shapes=[pltpu.SMEM((TOP_K * tm,), jnp.int32),
                        pltpu.VMEM((TOP_K, tm, D), F32),
                        pltpu.SemaphoreType.DMA(()), pltpu.SemaphoreType.DMA(())],
        compiler_params=_params("arbitrary"),
        name="moe_combine",
    )(dest, ys, x, meta)


def _moe_residual(x, g, w_router, e_gate, e_up, e_down, *, tm_route, tm_move, bm, tf):
    T, D = x.shape
    meta, counts = _router(x, g, w_router, tm=tm_route)
    counts = counts[0, :N_EXPERTS].astype(jnp.int32)
    padded = (counts + bm - 1) // bm * bm
    pend = jnp.cumsum(padded)
    pstart = pend - padded
    expert = meta[:, 0:TOP_K].astype(jnp.int32)
    rank = meta[:, 4:4 + TOP_K].astype(jnp.int32)
    dest = (pstart[expert] + rank).reshape(T * TOP_K)
    n_rows = T * TOP_K + N_EXPERTS * bm
    nblk = n_rows // bm
    blk_row0 = jnp.arange(nblk, dtype=jnp.int32) * bm
    blk_expert = jnp.minimum(jnp.searchsorted(pend, blk_row0, side='right'),
                             N_EXPERTS - 1).astype(jnp.int32)
    blk_valid = jnp.clip(counts[blk_expert] - (blk_row0 - pstart[blk_expert]), 0, bm)
    blk_valid = jnp.where(blk_row0 < pend[-1], blk_valid, 0).astype(jnp.int32)
    last_used = jnp.maximum(pend[-1] // bm - 1, 0)
    blk_expert = jnp.where(blk_row0 < pend[-1], blk_expert, blk_expert[last_used])

    xs = _dispatch(dest, x, n_rows, tm=tm_move)
    ys = _expert_ffn(xs, g, e_gate, e_up, e_down, blk_expert, blk_valid, bm=bm, tf=tf)
    return _combine(dest, ys, x, meta, tm=tm_move)


def _alibi_slopes(n_heads):
    return 2.0 ** (-8.0 * jnp.arange(1, n_heads + 1, dtype=F32) / n_heads)


def _tile(n, want):
    t = min(n, want)
    while n % t:
        t //= 2
    return t


Q_SCALE = HEAD_DIM ** -0.5 * LOG2E
PROJ_TN = 512


BOUNDED_SCORE_LIMIT = 60.0


def _score_bound(q_gain, k_gain):
    return 1.01 * HEAD_DIM * Q_SCALE * jnp.max(jnp.abs(q_gain)) * jnp.max(jnp.abs(k_gain))


def _head_kinds(head_ops):
    hpt = PROJ_TN // HEAD_DIM
    return tuple((t, t + 1, tuple(head_ops[t * hpt:(t + 1) * hpt]))
                 for t in range(len(head_ops) // hpt))


def _layer0(x, norm_mix, w_in, qn_a, kn_a, qn_b, kn_b, w_out, norm_ffn, w_gate, w_up, w_down):
    B, S, D = x.shape
    T = B * S
    xf = x.reshape(T, D)
    tm = _tile(S, 1024)
    a_q, a_kv, b_dim = A_HEADS * HEAD_DIM, A_KV_HEADS * HEAD_DIM, B_HEADS * HEAD_DIM
    head_gain = jnp.concatenate([
        jnp.tile(qn_a, A_HEADS), jnp.tile(kn_a, A_KV_HEADS), jnp.ones((a_kv,), F32),
        jnp.tile(qn_b, B_HEADS), jnp.tile(kn_b, B_HEADS), jnp.ones((b_dim,), F32)])
    seg = [(a_q, (True, True, Q_SCALE)), (a_kv, (True, True, 1.0)), (a_kv, (False, False, 1.0)),
           (b_dim, (True, False, Q_SCALE)), (b_dim, (True, False, 1.0)), (b_dim, (False, False, 1.0))]
    head_ops = [op for width, op in seg for _ in range(width // HEAD_DIM)]
    proj = _normproj(xf, norm_mix, w_in.astype(BF16), head_gain, _head_kinds(head_ops),
                     _rope_tables(S), tm=tm, tn=PROJ_TN)
    gqa = functools.partial(_gqa_attention, B=B, S=S, q_col0=0, k_col0=a_q, v_col0=a_q + a_kv)
    o_a = lax.cond(_score_bound(qn_a, kn_a) <= BOUNDED_SCORE_LIMIT,
                   lambda p: gqa(p, tq=_tile(S, 1024), tk=_tile(S, 2048), bounded=True),
                   lambda p: gqa(p, tq=_tile(S, 512), tk=_tile(S, 1024), bounded=False), proj)
    b0 = a_q + 2 * a_kv
    o_b = _dilated_attention(proj, B, S, _alibi_slopes(B_HEADS), q_col0=b0, k_col0=b0 + b_dim,
                             v_col0=b0 + 2 * b_dim, tq=_tile(S, 512))
    x1 = _outproj_residual([o_a, o_b], w_out.astype(BF16), xf, tm=tm, tn=512)
    x2 = _ffn_residual(x1, norm_ffn, w_gate.astype(BF16), w_up.astype(BF16), w_down.astype(BF16),
                       tm=_tile(T, 512), tf=512)
    return x2.reshape(B, S, D)


def _layer1(x, norm_mix, w_in, qn_c, kn_c, lam_q1, lam_k1, lam_q2, lam_k2, subln, w_out, norm_ffn,
            w_router, e_gate, e_up, e_down):
    B, S, D = x.shape
    T = B * S
    xf = x.reshape(T, D)
    tm = _tile(S, 1024)
    c_qk = C_HEADS * 2 * HEAD_DIM
    head_gain = jnp.concatenate([jnp.tile(qn_c, 2 * C_HEADS), jnp.tile(kn_c, 2 * C_HEADS),
                                 jnp.ones((c_qk,), F32)])
    head_ops = ([(True, False, Q_SCALE)] * (2 * C_HEADS) + [(True, False, 1.0)] * (2 * C_HEADS)
                + [(False, False, 1.0)] * (2 * C_HEADS))
    proj = _normproj(xf, norm_mix, w_in.astype(BF16), head_gain, _head_kinds(head_ops), None,
                     tm=tm, tn=PROJ_TN)
    lambda_init = 0.8 - 0.6 * math.exp(-0.3 * 1)
    diff = functools.partial(_diff_attention, B=B, S=S, slopes=_alibi_slopes(C_HEADS),
                             lam_vecs=(lam_q1, lam_k1, lam_q2, lam_k2), subln=subln,
                             lambda_init=lambda_init)
    o_c = lax.cond(_score_bound(qn_c, kn_c) <= BOUNDED_SCORE_LIMIT,
                   lambda p: diff(p, tq=_tile(S, 1024), tk=_tile(S, 2048), bounded=True),
                   lambda p: diff(p, tq=_tile(S, 512), tk=_tile(S, 1024), bounded=False), proj)
    x3 = _outproj_residual([o_c], w_out.astype(BF16), xf, tm=tm, tn=512)
    out = _moe_residual(x3, norm_ffn, w_router, e_gate.astype(BF16), e_up.astype(BF16),
                        e_down.astype(BF16), tm_route=_tile(T, 512), tm_move=_tile(T, 512),
                        bm=_tile(T, 512), tf=512)
    return out.reshape(B, S, D)


def kernel(x, l0_norm_mix,l0_w_in, l0_qnorm_a, l0_knorm_a, l0_qnorm_b, l0_knorm_b, l0_w_out, l0_norm_ffn, l0_w_gate, l0_w_up, l0_w_down, l1_norm_mix, l1_w_in, l1_qnorm_c, l1_knorm_c, l1_lambda_q1, l1_lambda_k1, l1_lambda_q2, l1_lambda_k2, l1_subln, l1_w_out, l1_norm_ffn, l1_w_router, l1_e_gate, l1_e_up, l1_e_down):
    x = _layer0(x, l0_norm_mix, l0_w_in, l0_qnorm_a, l0_knorm_a, l0_qnorm_b, l0_knorm_b, l0_w_out,
                l0_norm_ffn, l0_w_gate, l0_w_up, l0_w_down)
    return _layer1(x, l1_norm_mix, l1_w_in, l1_qnorm_c, l1_knorm_c, l1_lambda_q1, l1_lambda_k1,
                   l1_lambda_q2, l1_lambda_k2, l1_subln, l1_w_out, l1_norm_ffn, l1_w_router,
                   l1_e_gate, l1_e_up, l1_e_down)
```

```python
import functools
import math

import jax
import jax.numpy as jnp
from jax import lax
from jax.experimental import pallas as pl
from jax.experimental.pallas import tpu as pltpu

F32 = jnp.float32
BF16 = jnp.bfloat16

HEAD_DIM = 128
LANES = 128
A_HEADS = 8
A_KV_HEADS = 2
B_HEADS = 8
B_PATTERNS = ((128, 1), (512, 4), (2048, 16))
C_HEADS = 8
N_EXPERTS = 8
TOP_K = 2
GRID_W = 64
ROPE_THETA = 10000.0
NORM_EPS = 1e-6
NEG_INF = -1e30
LOG2E = 1.4426950408889634
VMEM_LIMIT_BYTES = 56 * 1024 * 1024

_NT = (((1,), (1,)), ((), ()))


def _params(*sem):
    return pltpu.CompilerParams(dimension_semantics=sem, vmem_limit_bytes=VMEM_LIMIT_BYTES)


def _rms(x, eps=NORM_EPS):
    return x * lax.rsqrt(jnp.mean(x * x, axis=-1, keepdims=True) + eps)


PROJ_ROW_CHUNKS = 2


def _normproj_kernel(*refs, heads_per_tile, has_rope):
    if has_rope:
        x_ref, g_ref, w_ref, cols_ref, cos_ref, sa_ref, sb_ref, o_ref, xn_ref = refs
    else:
        x_ref, g_ref, w_ref, cols_ref, o_ref, xn_ref = refs

    @pl.when(pl.program_id(1) == 0)
    def _():
        xn_ref[...] = (_rms(x_ref[...]) * g_ref[...]).astype(BF16)

    rows = x_ref.shape[0] // PROJ_ROW_CHUNKS
    for r in range(PROJ_ROW_CHUNKS):
        rs = slice(r * rows, (r + 1) * rows)
        acc = jnp.dot(xn_ref[rs, :], w_ref[...], preferred_element_type=F32)
        for h in range(heads_per_tile):
            sl = slice(h * HEAD_DIM, (h + 1) * HEAD_DIM)
            y = acc[:, sl]
            inv = lax.rsqrt(jnp.mean(y * y, axis=-1, keepdims=True) + NORM_EPS)
            y = y * (inv * cols_ref[0:1, sl] + cols_ref[1:2, sl])
            if has_rope:
                rot = (y * cos_ref[rs, :] + pltpu.roll(y, 96, 1) * sa_ref[rs, :]
                       + pltpu.roll(y, 32, 1) * sb_ref[rs, :])
                y = y + cols_ref[2:3, sl] * (rot - y)
            o_ref[rs, sl] = (y * cols_ref[3:4, sl]).astype(o_ref.dtype)


def _normproj(x, g, w, head_ops, gains, rope_tables, *, col0, tm, tn):
    T, D = x.shape
    N = len(head_ops) * HEAD_DIM
    has_rope = rope_tables is not None
    zero, one = jnp.zeros((HEAD_DIM,), F32), jnp.ones((HEAD_DIM,), F32)
    cols = jnp.stack([
        jnp.concatenate([zero if gi is None else gains[gi] for gi, _, _ in head_ops]),
        jnp.concatenate([one if gi is None else zero for gi, _, _ in head_ops]),
        jnp.concatenate([one if rope else zero for _, rope, _ in head_ops]),
        jnp.concatenate([one * scale for _, _, scale in head_ops])])
    cols = jnp.concatenate([cols, jnp.zeros((4, N), F32)])
    jt0 = col0 // tn
    in_specs = [
        pl.BlockSpec((tm, D), lambda i, j: (i, 0)),
        pl.BlockSpec((1, D), lambda i, j: (0, 0)),
        pl.BlockSpec((D, tn), lambda i, j: (0, jt0 + j)),
        pl.BlockSpec((8, tn), lambda i, j: (0, j)),
    ]
    args = [x, g.reshape(1, D), w, cols]
    if has_rope:
        ns = rope_tables[0].shape[0] // tm
        for t in rope_tables:
            in_specs.append(pl.BlockSpec((tm, HEAD_DIM), lambda i, j: (i % ns, 0)))
            args.append(t)
    return pl.pallas_call(
        functools.partial(_normproj_kernel, heads_per_tile=tn // HEAD_DIM, has_rope=has_rope),
        grid=(T // tm, N // tn),
        in_specs=in_specs,
        out_specs=pl.BlockSpec((tm, tn), lambda i, j: (i, j)),
        out_shape=jax.ShapeDtypeStruct((T, N), BF16),
        scratch_shapes=[pltpu.VMEM((tm, D), BF16)],
        compiler_params=_params("parallel", "arbitrary"),
        name="normproj_rope" if has_rope else "normproj",
    )(*args)


def _rope_tables(S):
    rows = S // GRID_W
    pos = jnp.arange(S, dtype=jnp.int32)
    row = (pos // GRID_W).astype(F32)
    col = (pos % GRID_W).astype(F32)
    del rows
    nf = HEAD_DIM // 4
    inv = ROPE_THETA ** (-jnp.arange(nf, dtype=F32) / nf)
    ang_row = row[:, None] * inv
    ang_col = col[:, None] * inv
    ang = jnp.concatenate([ang_row, ang_row, ang_col, ang_col], axis=-1)
    cos = jnp.cos(ang)
    sin = jnp.sin(ang)
    quarter = (jnp.arange(HEAD_DIM) // nf) % 2
    sa = jnp.where(quarter[None, :] == 0, -sin, 0.0)
    sb = jnp.where(quarter[None, :] == 1, sin, 0.0)
    return cos, sa, sb


def _gqa_kernel(q_ref, k_ref, v_ref, o_ref, m_ref, l_ref, acc_ref, *, group, nk):
    j = pl.program_id(3)

    @pl.when(j == 0)
    def _():
        m_ref[...] = jnp.full(m_ref.shape, NEG_INF, F32)
        l_ref[...] = jnp.zeros(l_ref.shape, F32)
        acc_ref[...] = jnp.zeros(acc_ref.shape, F32)

    k = k_ref[...]
    v = v_ref[...]
    for h in range(group):
        q = q_ref[:, h * HEAD_DIM:(h + 1) * HEAD_DIM]
        s = lax.dot_general(q, k, _NT, preferred_element_type=F32)
        m_prev = m_ref[h]
        m_new = jnp.maximum(m_prev, jnp.max(s, axis=1, keepdims=True))
        alpha = jnp.exp2(m_prev - m_new)
        p = jnp.exp2(s - m_new[:, :1])
        l_ref[h] = alpha * l_ref[h] + jnp.sum(p, axis=1, keepdims=True)
        acc_ref[h] = acc_ref[h] * alpha + jnp.dot(p.astype(BF16), v, preferred_element_type=F32)
        m_ref[h] = m_new

    @pl.when(j == nk - 1)
    def _():
        for h in range(group):
            o_ref[:, h * HEAD_DIM:(h + 1) * HEAD_DIM] = (acc_ref[h] / l_ref[h]).astype(o_ref.dtype)


def _gqa_bounded_kernel(q_ref, k_ref, v_ref, o_ref, l_ref, acc_ref, *, group, nk):
    j = pl.program_id(3)

    @pl.when(j == 0)
    def _():
        l_ref[...] = jnp.zeros(l_ref.shape, F32)
        acc_ref[...] = jnp.zeros(acc_ref.shape, F32)

    k = k_ref[...]
    v = v_ref[...]
    for h in range(group):
        q = q_ref[:, h * HEAD_DIM:(h + 1) * HEAD_DIM]
        p = jnp.exp2(lax.dot_general(q, k, _NT, preferred_element_type=F32))
        l_ref[h] += jnp.sum(p, axis=1, keepdims=True)
        acc_ref[h] += jnp.dot(p.astype(BF16), v, preferred_element_type=F32)

    @pl.when(j == nk - 1)
    def _():
        for h in range(group):
            o_ref[:, h * HEAD_DIM:(h + 1) * HEAD_DIM] = (acc_ref[h] / l_ref[h]).astype(o_ref.dtype)


def _gqa_attention(proj, B, S, *, q_col0, k_col0, v_col0, tq, tk, bounded):
    group = A_HEADS // A_KV_HEADS
    nq, nk = S // tq, S // tk
    gw = group * HEAD_DIM
    if bounded:
        body = functools.partial(_gqa_bounded_kernel, group=group, nk=nk)
        scratch = [pltpu.VMEM((group, tq, LANES), F32), pltpu.VMEM((group, tq, HEAD_DIM), F32)]
    else:
        body = functools.partial(_gqa_kernel, group=group, nk=nk)
        scratch = [pltpu.VMEM((group, tq, LANES), F32), pltpu.VMEM((group, tq, LANES), F32),
                   pltpu.VMEM((group, tq, HEAD_DIM), F32)]
    return pl.pallas_call(
        body,
        grid=(B, A_KV_HEADS, nq, nk),
        in_specs=[
            pl.BlockSpec((tq, gw), lambda b, g, i, j: (b * nq + i, q_col0 // gw + g)),
            pl.BlockSpec((tk, HEAD_DIM), lambda b, g, i, j: (b * nk + j, k_col0 // HEAD_DIM + g)),
            pl.BlockSpec((tk, HEAD_DIM), lambda b, g, i, j: (b * nk + j, v_col0 // HEAD_DIM + g)),
        ],
        out_specs=pl.BlockSpec((tq, gw), lambda b, g, i, j: (b * nq + i, g)),
        out_shape=jax.ShapeDtypeStruct((B * S, A_HEADS * HEAD_DIM), BF16),
        scratch_shapes=scratch,
        compiler_params=_params("parallel", "parallel", "parallel", "arbitrary"),
        name="gqa_bounded" if bounded else "gqa_attention",
    )(proj, proj, proj)


def _dilated_kernel(q_ref, *refs, nside, nq):
    nb = 2 * nside + 1
    k_refs, v_refs = refs[:nb], refs[nb:2 * nb]
    tbl_ref, o_ref = refs[2 * nb], refs[2 * nb + 1]
    i = pl.program_id(2)
    q = q_ref[...]
    scores = []
    m = None
    for d in range(nb):
        blk = i + (d - nside)
        in_range = (blk >= 0) & (blk < nq)
        s = lax.dot_general(q, k_refs[d][...], _NT, preferred_element_type=F32) + tbl_ref[0, d]
        s = jnp.where(in_range, s, NEG_INF)
        scores.append(s)
        md = jnp.max(s, axis=1, keepdims=True)
        m = md if m is None else jnp.maximum(m, md)
    l = None
    acc = None
    for d in range(nb):
        p = jnp.exp2(scores[d] - m)
        ld = jnp.sum(p, axis=1, keepdims=True)
        ad = jnp.dot(p.astype(BF16), v_refs[d][...], preferred_element_type=F32)
        l = ld if l is None else l + ld
        acc = ad if acc is None else acc + ad
    o_ref[...] = (acc / l).astype(o_ref.dtype)


def _dilated_bounded_kernel(q_ref, *refs, nside, nq):
    nb = 2 * nside + 1
    k_refs, v_refs = refs[:nb], refs[nb:2 * nb]
    tbl_ref, o_ref = refs[2 * nb], refs[2 * nb + 1]
    i = pl.program_id(2)
    q = q_ref[...]
    l = None
    acc = None
    for d in range(nb):
        blk = i + (d - nside)
        in_range = (blk >= 0) & (blk < nq)
        s = lax.dot_general(q, k_refs[d][...], _NT, preferred_element_type=F32) + tbl_ref[0, d]
        p = jnp.exp2(jnp.where(in_range, s, NEG_INF))
        ld = jnp.sum(p, axis=1, keepdims=True)
        ad = jnp.dot(p.astype(BF16), v_refs[d][...], preferred_element_type=F32)
        l = ld if l is None else l + ld
        acc = ad if acc is None else acc + ad
    o_ref[...] = (acc / l).astype(o_ref.dtype)


def _dilated_bias_table(slopes, tq, nside):
    nb = 2 * nside + 1
    a = jnp.arange(tq, dtype=jnp.int32)
    d = ((jnp.arange(nb, dtype=jnp.int32)[:, None, None] - nside) * tq
         + a[None, None, :] - a[None, :, None])
    ad = jnp.abs(d)
    count = jnp.zeros(d.shape, F32)
    for window, dil in B_PATTERNS:
        reach = (window // (2 * dil)) * dil
        count = count + ((ad % dil == 0) & (ad <= reach)).astype(F32)
    logc = jnp.where(count > 0, jnp.log2(jnp.maximum(count, 1.0)), NEG_INF)
    bias = -(slopes * LOG2E)[:, None, None, None] * ad.astype(F32)[None]
    return jnp.where(count[None] > 0, bias + logc[None], NEG_INF)


def _dilated_attention(proj, B, S, slopes, *, q_col0, k_col0, v_col0, tq, bounded):
    reach = max((w // (2 * dl)) * dl for w, dl in B_PATTERNS)
    nside = -(-reach // tq)
    nb = 2 * nside + 1
    nq = S // tq
    table = _dilated_bias_table(slopes, tq, nside)

    def kv_spec(col0, d):
        def imap(h, b, i):
            return (b * nq + jnp.clip(i + (d - nside), 0, nq - 1), col0 // HEAD_DIM + h)
        return pl.BlockSpec((tq, HEAD_DIM), imap)

    in_specs = [pl.BlockSpec((tq, HEAD_DIM), lambda h, b, i: (b * nq + i, q_col0 // HEAD_DIM + h))]
    in_specs += [kv_spec(k_col0, d) for d in range(nb)]
    in_specs += [kv_spec(v_col0, d) for d in range(nb)]
    in_specs += [pl.BlockSpec((1, nb, tq, tq), lambda h, b, i: (h, 0, 0, 0))]
    return pl.pallas_call(
        functools.partial(_dilated_bounded_kernel if bounded else _dilated_kernel,
                          nside=nside, nq=nq),
        grid=(B_HEADS, B, nq),
        in_specs=in_specs,
        out_specs=pl.BlockSpec((tq, HEAD_DIM), lambda h, b, i: (b * nq + i, h)),
        out_shape=jax.ShapeDtypeStruct((B * S, B_HEADS * HEAD_DIM), BF16),
        compiler_params=_params("parallel", "parallel", "parallel"),
        name="dilated_bounded" if bounded else "dilated_attention",
    )(*([proj] * (1 + 2 * nb)), table)


def _outproj_kernel(*refs, n_parts):
    a_refs = refs[:n_parts]
    w_refs = refs[n_parts:2 * n_parts]
    r_ref, o_ref = refs[2 * n_parts], refs[2 * n_parts + 1]
    acc = r_ref[...]
    for a_ref, w_ref in zip(a_refs, w_refs):
        acc = acc + jnp.dot(a_ref[...], w_ref[...], preferred_element_type=F32)
    o_ref[...] = acc


def _outproj_residual(parts, w, res, *, tm, tn):
    T, N = res.shape
    n_parts = len(parts)
    in_specs, w_args, off = [], [], 0
    for a in parts:
        kp = a.shape[1]
        in_specs.append(pl.BlockSpec((tm, kp), lambda i, j: (i, 0)))
        w_args.append((kp, off // kp))
        off += kp
    for kp, blk in w_args:
        in_specs.append(pl.BlockSpec((kp, tn), lambda i, j, blk=blk: (blk, j)))
    in_specs.append(pl.BlockSpec((tm, tn), lambda i, j: (i, j)))
    return pl.pallas_call(
        functools.partial(_outproj_kernel, n_parts=n_parts),
        grid=(T // tm, N // tn),
        in_specs=in_specs,
        out_specs=pl.BlockSpec((tm, tn), lambda i, j: (i, j)),
        out_shape=jax.ShapeDtypeStruct((T, N), F32),
        compiler_params=_params("parallel", "arbitrary"),
        name="outproj_residual",
    )(*parts, *([w] * n_parts), res)


def _silu(g):
    return g / (1.0 + jnp.exp(-g))


def _ffn_kernel(x_ref, g_ref, wg_ref, wu_ref, wd_ref, o_ref, xn_ref):
    @pl.when(pl.program_id(1) == 0)
    def _():
        x = x_ref[...]
        xn_ref[...] = (_rms(x) * g_ref[...]).astype(BF16)
        o_ref[...] = x

    xn = xn_ref[...]
    gate = jnp.dot(xn, wg_ref[...], preferred_element_type=F32)
    up = jnp.dot(xn, wu_ref[...], preferred_element_type=F32)
    hid = (_silu(gate) * up).astype(BF16)
    o_ref[...] += jnp.dot(hid, wd_ref[...], preferred_element_type=F32)


def _ffn_residual(x, g, w_gate, w_up, w_down, *, tm, tf):
    T, D = x.shape
    Fd = w_gate.shape[1]
    nf = Fd // tf
    return pl.pallas_call(
        _ffn_kernel,
        grid=(T // tm, nf),
        in_specs=[
            pl.BlockSpec((tm, D), lambda i, f: (i, 0), pipeline_mode=pl.Buffered(1)),
            pl.BlockSpec((1, D), lambda i, f: (0, 0)),
            pl.BlockSpec((D, tf), lambda i, f: (0, f)),
            pl.BlockSpec((D, tf), lambda i, f: (0, f)),
            pl.BlockSpec((tf, D), lambda i, f: (f, 0)),
        ],
        out_specs=pl.BlockSpec((tm, D), lambda i, f: (i, 0)),
        out_shape=jax.ShapeDtypeStruct((T, D), F32),
        scratch_shapes=[pltpu.VMEM((tm, D), BF16)],
        compiler_params=_params("parallel", "arbitrary"),
        name="ffn_residual",
    )(x, g.reshape(1, D), w_gate, w_up, w_down)


def _diff_kernel(slopes_ref, q_ref, k_ref, v_ref, dmat_ref, lq1_ref, lk1_ref, lq2_ref, lk2_ref,
                 subln_ref, o_ref, m_ref, l_ref, acc_ref, *, tq, tk, nk, lambda_init):
    h = pl.program_id(1)
    i = pl.program_id(2)
    j = pl.program_id(3)

    @pl.when(j == 0)
    def _():
        m_ref[...] = jnp.full(m_ref.shape, NEG_INF, F32)
        l_ref[...] = jnp.zeros(l_ref.shape, F32)
        acc_ref[...] = jnp.zeros(acc_ref.shape, F32)

    neg_slope = -slopes_ref[h] * LOG2E
    bias = _alibi_bias(dmat_ref[...], (j * tk - i * tq).astype(F32), neg_slope, tk)
    v = v_ref[...]
    for c in range(2):
        sl = slice(c * HEAD_DIM, (c + 1) * HEAD_DIM)
        s = lax.dot_general(q_ref[:, sl], k_ref[:, sl], _NT, preferred_element_type=F32) + bias
        m_prev = m_ref[c]
        m_new = jnp.maximum(m_prev, jnp.max(s, axis=1, keepdims=True))
        alpha = jnp.exp2(m_prev - m_new)
        p = jnp.exp2(s - m_new[:, :1])
        l_ref[c] = alpha * l_ref[c] + jnp.sum(p, axis=1, keepdims=True)
        acc_ref[c] = acc_ref[c] * alpha[:, :1] + jnp.dot(p.astype(BF16), v,
                                                         preferred_element_type=F32)
        m_ref[c] = m_new

    @pl.when(j == nk - 1)
    def _():
        _diff_finalize(acc_ref, l_ref, lq1_ref, lk1_ref, lq2_ref, lk2_ref, subln_ref, o_ref,
                       lambda_init)


def _alibi_bias(dbase, offset, neg_slope, tk):
    return jnp.concatenate([neg_slope * jnp.abs(dbase + (offset + float(cb * LANES)))
                            for cb in range(tk // LANES)], axis=1)


def _diff_finalize(acc_ref, l_ref, lq1_ref, lk1_ref, lq2_ref, lk2_ref, subln_ref, o_ref, lambda_init):
    lam = (jnp.exp(jnp.sum(lq1_ref[...] * lk1_ref[...], axis=-1, keepdims=True))
           - jnp.exp(jnp.sum(lq2_ref[...] * lk2_ref[...], axis=-1, keepdims=True))
           + lambda_init)
    o = acc_ref[0] / l_ref[0][:, :1] - lam * (acc_ref[1] / l_ref[1][:, :1])
    o = _rms(o) * subln_ref[...] * (1.0 - lambda_init)
    o_ref[...] = o.astype(o_ref.dtype)


def _diff_bounded_kernel(slopes_ref, q_ref, k_ref, v_ref, dmat_ref, lq1_ref, lk1_ref, lq2_ref,
                         lk2_ref, subln_ref, o_ref, l_ref, acc_ref, *, tq, tk, nk, lambda_init):
    h = pl.program_id(1)
    i = pl.program_id(2)
    j = pl.program_id(3)

    @pl.when(j == 0)
    def _():
        l_ref[...] = jnp.zeros(l_ref.shape, F32)
        acc_ref[...] = jnp.zeros(acc_ref.shape, F32)

    neg_slope = -slopes_ref[h] * LOG2E
    q0 = i * tq
    k0 = j * tk
    v = v_ref[...]

    def scores(c):
        sl = slice(c * HEAD_DIM, (c + 1) * HEAD_DIM)
        return lax.dot_general(q_ref[:, sl], k_ref[:, sl], _NT, preferred_element_type=F32)

    overlaps = (k0 < q0 + tq) & (q0 < k0 + tk)

    @pl.when(overlaps)
    def _():
        for c in range(2):
            p = jnp.exp2(scores(c) + _alibi_bias(dmat_ref[...], (k0 - q0).astype(F32), neg_slope, tk))
            l_ref[c] += jnp.sum(p, axis=1, keepdims=True)
            acc_ref[c] += jnp.dot(p.astype(BF16), v, preferred_element_type=F32)

    @pl.when(jnp.logical_not(overlaps))
    def _():
        right = k0 >= q0 + tq
        gap = jnp.where(right, k0 - (q0 + tq), q0 - (k0 + tk)).astype(F32)
        key = lax.broadcasted_iota(jnp.int32, (1, tk), 1)
        qry = lax.broadcasted_iota(jnp.int32, (tq, 1), 0)
        key_bias = neg_slope * jnp.where(right, key, tk - key).astype(F32)
        qry_fac = jnp.exp2(neg_slope * (jnp.where(right, tq - qry, qry).astype(F32) + gap))
        for c in range(2):
            p = jnp.exp2(scores(c) + key_bias)
            l_ref[c] += qry_fac * jnp.sum(p, axis=1, keepdims=True)
            acc_ref[c] += qry_fac * jnp.dot(p.astype(BF16), v, preferred_element_type=F32)

    @pl.when(j == nk - 1)
    def _():
        _diff_finalize(acc_ref, l_ref, lq1_ref, lk1_ref, lq2_ref, lk2_ref, subln_ref, o_ref,
                       lambda_init)


def _diff_attention(proj, B, S, slopes, lam_vecs, subln, lambda_init, *, tq, tk, bounded):
    H = C_HEADS
    hw = 2 * HEAD_DIM
    nq, nk = S // tq, S // tk
    dmat = (jnp.arange(LANES, dtype=F32)[None, :] - jnp.arange(tq, dtype=F32)[:, None])
    stat = [pltpu.VMEM((2, tq, LANES), F32)]
    if bounded:
        body, name = _diff_bounded_kernel, "diff_bounded"
    else:
        body, name, stat = _diff_kernel, "diff_attention", stat * 2
    vec_spec = pl.BlockSpec((1, HEAD_DIM), lambda b, h, i, j, s: (0, 0))
    grid_spec = pltpu.PrefetchScalarGridSpec(
        num_scalar_prefetch=1,
        grid=(B, H, nq, nk),
        in_specs=[
            pl.BlockSpec((tq, hw), lambda b, h, i, j, s: (b * nq + i, h)),
            pl.BlockSpec((tk, hw), lambda b, h, i, j, s: (b * nk + j, H + h)),
            pl.BlockSpec((tk, hw), lambda b, h, i, j, s: (b * nk + j, 2 * H + h)),
            pl.BlockSpec((tq, LANES), lambda b, h, i, j, s: (0, 0)),
            vec_spec, vec_spec, vec_spec, vec_spec,
            pl.BlockSpec((1, hw), lambda b, h, i, j, s: (0, 0)),
        ],
        out_specs=pl.BlockSpec((tq, hw), lambda b, h, i, j, s: (b * nq + i, h)),
        scratch_shapes=stat + [pltpu.VMEM((2, tq, hw), F32)],
    )
    return pl.pallas_call(
        functools.partial(body, tq=tq, tk=tk, nk=nk, lambda_init=lambda_init),
        grid_spec=grid_spec,
        out_shape=jax.ShapeDtypeStruct((B * S, H * hw), BF16),
        compiler_params=_params("parallel", "parallel", "parallel", "arbitrary"),
        name=name,
    )(slopes, proj, proj, proj, dmat, *[v.reshape(1, HEAD_DIM) for v in lam_vecs],
      subln.reshape(1, hw))


def _router_kernel(x_ref, g_ref, wr_ref, meta_ref, cnt_ref, base_ref, *, tm, n_exp):
    i = pl.program_id(0)

    @pl.when(i == 0)
    def _():
        base_ref[...] = jnp.zeros(base_ref.shape, F32)

    hn = _rms(x_ref[...]) * g_ref[...]
    logits = jnp.dot(hn, wr_ref[...], preferred_element_type=F32,
                     precision=lax.Precision.HIGHEST)
    lane = lax.broadcasted_iota(jnp.int32, (tm, LANES), 1)
    logits = jnp.where(lane < n_exp, logits, NEG_INF)
    t1 = jnp.max(logits, axis=1, keepdims=True)
    i1 = jnp.min(jnp.where(logits == t1, lane, LANES), axis=1, keepdims=True)
    rest = jnp.where(lane == i1, NEG_INF, logits)
    t2 = jnp.max(rest, axis=1, keepdims=True)
    i2 = jnp.min(jnp.where(rest == t2, lane, LANES), axis=1, keepdims=True)
    ex = jnp.exp(t2 - t1)
    g1 = 1.0 / (1.0 + ex)
    g2 = ex * g1
    oh1 = lane == i1
    oh2 = lane == i2
    member = jnp.where(oh1 | oh2, 1.0, 0.0)
    row = lax.broadcasted_iota(jnp.int32, (tm, tm), 0)
    col = lax.broadcasted_iota(jnp.int32, (tm, tm), 1)
    strict_lower = jnp.where(col < row, 1.0, 0.0).astype(BF16)
    before = jnp.dot(strict_lower, member.astype(BF16), preferred_element_type=F32) + base_ref[...]
    r1 = jnp.sum(jnp.where(oh1, before, 0.0), axis=1, keepdims=True)
    r2 = jnp.sum(jnp.where(oh2, before, 0.0), axis=1, keepdims=True)
    base_ref[...] += jnp.sum(member, axis=0, keepdims=True)
    meta = jnp.where(lane == 0, i1.astype(F32), 0.0)
    meta = jnp.where(lane == 1, i2.astype(F32), meta)
    meta = jnp.where(lane == 2, g1, meta)
    meta = jnp.where(lane == 3, g2, meta)
    meta = jnp.where(lane == 4, r1, meta)
    meta = jnp.where(lane == 5, r2, meta)
    meta_ref[...] = meta
    cnt_ref[...] = base_ref[...]


def _router(x, g, w_router, *, tm):
    T, D = x.shape
    n_exp = w_router.shape[1]
    wr = jnp.zeros((D, LANES), F32).at[:, :n_exp].set(w_router)
    return pl.pallas_call(
        functools.partial(_router_kernel, tm=tm, n_exp=n_exp),
        grid=(T // tm,),
        in_specs=[
            pl.BlockSpec((tm, D), lambda i: (i, 0)),
            pl.BlockSpec((1, D), lambda i: (0, 0)),
            pl.BlockSpec((D, LANES), lambda i: (0, 0)),
        ],
        out_specs=[pl.BlockSpec((tm, LANES), lambda i: (i, 0)),
                   pl.BlockSpec((1, LANES), lambda i: (0, 0))],
        out_shape=[jax.ShapeDtypeStruct((T, LANES), F32),
                   jax.ShapeDtypeStruct((1, LANES), F32)],
        scratch_shapes=[pltpu.VMEM((1, LANES), F32)],
        compiler_params=_params("arbitrary"),
        name="moe_router",
    )(x, g.reshape(1, D), wr)


def _row_copy(src_ref, src_row, dst_ref, dst_row, sem):
    return pltpu.make_async_copy(src_ref.at[pl.ds(src_row, 1)], dst_ref.at[pl.ds(dst_row, 1)], sem)


def _dispatch_kernel(dest_hbm, x_ref, zeros_hbm, xs_hbm, idx_ref, idx_sem, row_sem, *, tm):
    del zeros_hbm
    i = pl.program_id(0)
    n_idx = TOP_K * tm
    idx_copy = pltpu.make_async_copy(dest_hbm.at[pl.ds(i * n_idx, n_idx)], idx_ref, idx_sem)
    idx_copy.start()
    idx_copy.wait()

    def issue(t, carry):
        for k in range(TOP_K):
            _row_copy(x_ref, t, xs_hbm, idx_ref[TOP_K * t + k], row_sem).start()
        return carry

    lax.fori_loop(0, tm, issue, 0)

    for k in range(TOP_K):
        pltpu.make_async_copy(x_ref, xs_hbm.at[pl.ds(0, tm)], row_sem).wait()


def _dispatch(dest, x, n_rows, *, tm):
    T, D = x.shape
    return pl.pallas_call(
        functools.partial(_dispatch_kernel, tm=tm),
        grid=(T // tm,),
        in_specs=[pl.BlockSpec(memory_space=pl.ANY),
                  pl.BlockSpec((tm, D), lambda i: (i, 0)),
                  pl.BlockSpec(memory_space=pl.ANY)],
        out_specs=pl.BlockSpec(memory_space=pl.ANY),
        out_shape=jax.ShapeDtypeStruct((n_rows, D), F32),
        input_output_aliases={2: 0},
        scratch_shapes=[pltpu.SMEM((TOP_K * tm,), jnp.int32),
                        pltpu.SemaphoreType.DMA(()), pltpu.SemaphoreType.DMA(())],
        compiler_params=pltpu.CompilerParams(dimension_semantics=("arbitrary",),
                                             vmem_limit_bytes=VMEM_LIMIT_BYTES,
                                             has_side_effects=True),
        name="moe_dispatch",
    )(dest, x, jnp.zeros((n_rows, D), F32))


def _expert_kernel(be_ref, bn_ref, x_ref, g_ref, wg_ref, wu_ref, wd_ref, y_ref, xn_ref, *, bm):
    del be_ref
    n_valid = bn_ref[pl.program_id(0)]

    @pl.when(pl.program_id(1) == 0)
    def _():
        rows = lax.broadcasted_iota(jnp.int32, (bm, 1), 0)
        x = jnp.where(rows < n_valid, x_ref[...], 0.0)
        xn_ref[...] = (_rms(x) * g_ref[...]).astype(BF16)
        y_ref[...] = jnp.zeros(y_ref.shape, F32)

    def ffn_rows(rs):
        xn = xn_ref[rs, :]
        gate = jnp.dot(xn, wg_ref[...], preferred_element_type=F32)
        up = jnp.dot(xn, wu_ref[...], preferred_element_type=F32)
        hid = (_silu(gate) * up).astype(BF16)
        y_ref[rs, :] += jnp.dot(hid, wd_ref[...], preferred_element_type=F32)

    half = bm // 2

    @pl.when(n_valid > half)
    def _():
        ffn_rows(slice(0, bm))

    @pl.when((n_valid > 0) & (n_valid <= half))
    def _():
        ffn_rows(slice(0, half))


def _expert_ffn(xs, g, e_gate, e_up, e_down, blk_expert, blk_valid, *, bm, tf):
    P, D = xs.shape
    Fd = e_gate.shape[2]
    nf = Fd // tf
    nblk = P // bm

    def f_idx(b, f, bn):
        return jnp.where(bn[b] > 0, f, nf - 1)

    grid_spec = pltpu.PrefetchScalarGridSpec(
        num_scalar_prefetch=2,
        grid=(nblk, nf),
        in_specs=[
            pl.BlockSpec((bm, D), lambda b, f, be, bn: (b, 0), pipeline_mode=pl.Buffered(1)),
            pl.BlockSpec((1, D), lambda b, f, be, bn: (0, 0)),
            pl.BlockSpec((None, D, tf), lambda b, f, be, bn: (be[b], 0, f_idx(b, f, bn))),
            pl.BlockSpec((None, D, tf), lambda b, f, be, bn: (be[b], 0, f_idx(b, f, bn))),
            pl.BlockSpec((None, tf, D), lambda b, f, be, bn: (be[b], f_idx(b, f, bn), 0)),
        ],
        out_specs=pl.BlockSpec((bm, D), lambda b, f, be, bn: (b, 0)),
        scratch_shapes=[pltpu.VMEM((bm, D), BF16)],
    )
    return pl.pallas_call(
        functools.partial(_expert_kernel, bm=bm),
        grid_spec=grid_spec,
        out_shape=jax.ShapeDtypeStruct((P, D), F32),
        compiler_params=_params("arbitrary", "arbitrary"),
        name="moe_experts",
    )(blk_expert, blk_valid, xs, g.reshape(1, D), e_gate, e_up, e_down)


def _combine_kernel(dest_hbm, ys_hbm, x_ref, meta_ref, o_ref, idx_ref, ybuf_ref, idx_sem, row_sem,
                    *, tm):
    i = pl.program_id(0)
    n_idx = TOP_K * tm
    idx_copy = pltpu.make_async_copy(dest_hbm.at[pl.ds(i * n_idx, n_idx)], idx_ref, idx_sem)
    idx_copy.start()
    idx_copy.wait()

    def issue(t, carry):
        for k in range(TOP_K):
            _row_copy(ys_hbm, idx_ref[TOP_K * t + k], ybuf_ref.at[k], t, row_sem).start()
        return carry

    lax.fori_loop(0, tm, issue, 0)

    for k in range(TOP_K):
        pltpu.make_async_copy(ys_hbm.at[pl.ds(0, tm)], ybuf_ref.at[k], row_sem).wait()

    meta = meta_ref[...]
    out = x_ref[...]
    for k in range(TOP_K):
        out = out + meta[:, 2 + k:3 + k] * ybuf_ref[k]
    o_ref[...] = out


def _combine(dest, ys, x, meta, *, tm):
    T, D = x.shape
    return pl.pallas_call(
        functools.partial(_combine_kernel, tm=tm),
        grid=(T // tm,),
        in_specs=[pl.BlockSpec(memory_space=pl.ANY), pl.BlockSpec(memory_space=pl.ANY),
                  pl.BlockSpec((tm, D), lambda i: (i, 0)),
                  pl.BlockSpec((tm, LANES), lambda i: (i, 0))],
        out_specs=pl.BlockSpec((tm, D), lambda i: (i, 0)),
        out_shape=jax.ShapeDtypeStruct((T, D), F32),
        scratch_shapes=[pltpu.SMEM((TOP_K * tm,), jnp.int32),
                        pltpu.VMEM((TOP_K, tm, D), F32),
                        pltpu.SemaphoreType.DMA(()), pltpu.SemaphoreType.DMA(())],
        compiler_params=_params("arbitrary"),
        name="moe_combine",
    )(dest, ys, x, meta)


def _moe_residual(x, g, w_router, e_gate, e_up, e_down, *, tm_route, tm_move, bm, tf):
    T, D = x.shape
    meta, counts = _router(x, g, w_router, tm=tm_route)
    counts = counts[0, :N_EXPERTS].astype(jnp.int32)
    padded = (counts + bm - 1) // bm * bm
    pend = jnp.cumsum(padded)
    pstart = pend - padded
    expert = meta[:, 0:TOP_K].astype(jnp.int32)
    rank = meta[:, 4:4 + TOP_K].astype(jnp.int32)
    dest = (pstart[expert] + rank).reshape(T * TOP_K)
    n_rows = T * TOP_K + N_EXPERTS * bm
    nblk = n_rows // bm
    blk_row0 = jnp.arange(nblk, dtype=jnp.int32) * bm
    blk_expert = jnp.minimum(jnp.searchsorted(pend, blk_row0, side='right'),
                             N_EXPERTS - 1).astype(jnp.int32)
    blk_valid = jnp.clip(counts[blk_expert] - (blk_row0 - pstart[blk_expert]), 0, bm)
    blk_valid = jnp.where(blk_row0 < pend[-1], blk_valid, 0).astype(jnp.int32)
    last_used = jnp.maximum(pend[-1] // bm - 1, 0)
    blk_expert = jnp.where(blk_row0 < pend[-1], blk_expert, blk_expert[last_used])

    xs = _dispatch(dest, x, n_rows, tm=tm_move)
    ys = _expert_ffn(xs, g, e_gate, e_up, e_down, blk_expert, blk_valid, bm=bm, tf=tf)
    return _combine(dest, ys, x, meta, tm=tm_move)


def _alibi_slopes(n_heads):
    return 2.0 ** (-8.0 * jnp.arange(1, n_heads + 1, dtype=F32) / n_heads)


def _tile(n, want):
    t = min(n, want)
    while n % t:
        t //= 2
    return t


Q_SCALE = HEAD_DIM ** -0.5 * LOG2E
PROJ_TN = 512


BOUNDED_SCORE_LIMIT = 60.0


def _score_bound(q_gain, k_gain):
    return 1.01 * HEAD_DIM * Q_SCALE * jnp.max(jnp.abs(q_gain)) * jnp.max(jnp.abs(k_gain))


def _layer0(x, norm_mix, w_in, qn_a, kn_a, qn_b, kn_b, w_out, norm_ffn, w_gate, w_up, w_down):
    B, S, D = x.shape
    T = B * S
    xf = x.reshape(T, D)
    tm = _tile(S, 1024)
    a_q, a_kv, b_dim = A_HEADS * HEAD_DIM, A_KV_HEADS * HEAD_DIM, B_HEADS * HEAD_DIM
    w_in = w_in.astype(BF16)
    ops_a = ([(0, True, Q_SCALE)] * A_HEADS + [(1, True, 1.0)] * A_KV_HEADS
             + [(None, False, 1.0)] * A_KV_HEADS)
    ops_b = [(0, False, Q_SCALE)] * B_HEADS + [(1, False, 1.0)] * B_HEADS + [(None, False, 1.0)] * B_HEADS
    proj_a = _normproj(xf, norm_mix, w_in, ops_a, [qn_a, kn_a], _rope_tables(S), col0=0,
                       tm=tm, tn=PROJ_TN)
    proj_b = _normproj(xf, norm_mix, w_in, ops_b, [qn_b, kn_b], None, col0=a_q + 2 * a_kv,
                       tm=tm, tn=PROJ_TN)
    gqa = functools.partial(_gqa_attention, B=B, S=S, q_col0=0, k_col0=a_q, v_col0=a_q + a_kv)
    o_a = lax.cond(_score_bound(qn_a, kn_a) <= BOUNDED_SCORE_LIMIT,
                   lambda p: gqa(p, tq=_tile(S, 1024), tk=_tile(S, 2048), bounded=True),
                   lambda p: gqa(p, tq=_tile(S, 512), tk=_tile(S, 1024), bounded=False), proj_a)
    dil = functools.partial(_dilated_attention, B=B, S=S, slopes=_alibi_slopes(B_HEADS), q_col0=0,
                            k_col0=b_dim, v_col0=2 * b_dim, tq=_tile(S, 512))
    o_b = lax.cond(_score_bound(qn_b, kn_b) <= BOUNDED_SCORE_LIMIT,
                   lambda p: dil(p, bounded=True), lambda p: dil(p, bounded=False), proj_b)
    x1 = _outproj_residual([o_a, o_b], w_out.astype(BF16), xf, tm=tm, tn=512)
    x2 = _ffn_residual(x1, norm_ffn, w_gate.astype(BF16), w_up.astype(BF16), w_down.astype(BF16),
                       tm=_tile(T, 1024), tf=512)
    return x2.reshape(B, S, D)


def _layer1(x, norm_mix, w_in, qn_c, kn_c, lam_q1, lam_k1, lam_q2, lam_k2, subln, w_out, norm_ffn,
            w_router, e_gate, e_up, e_down):
    B, S, D = x.shape
    T = B * S
    xf = x.reshape(T, D)
    tm = _tile(S, 1024)
    head_ops = ([(0, False, Q_SCALE)] * (2 * C_HEADS) + [(1, False, 1.0)] * (2 * C_HEADS)
                + [(None, False, 1.0)] * (2 * C_HEADS))
    proj = _normproj(xf, norm_mix, w_in.astype(BF16), head_ops, [qn_c, kn_c], None, col0=0,
                     tm=tm, tn=PROJ_TN)
    lambda_init = 0.8 - 0.6 * math.exp(-0.3 * 1)
    diff = functools.partial(_diff_attention, B=B, S=S, slopes=_alibi_slopes(C_HEADS),
                             lam_vecs=(lam_q1, lam_k1, lam_q2, lam_k2), subln=subln,
                             lambda_init=lambda_init)
    o_c = lax.cond(_score_bound(qn_c, kn_c) <= BOUNDED_SCORE_LIMIT,
                   lambda p: diff(p, tq=_tile(S, 1024), tk=_tile(S, 2048), bounded=True),
                   lambda p: diff(p, tq=_tile(S, 512), tk=_tile(S, 1024), bounded=False), proj)
    x3 = _outproj_residual([o_c], w_out.astype(BF16), xf, tm=tm, tn=512)
    out = _moe_residual(x3, norm_ffn, w_router, e_gate.astype(BF16), e_up.astype(BF16),
                        e_down.astype(BF16), tm_route=_tile(T, 512), tm_move=_tile(T, 512),
                        bm=_tile(T, 1024), tf=512)
    return out.reshape(B, S, D)


def kernel(x, l0_norm_mix,l0_w_in, l0_qnorm_a, l0_knorm_a, l0_qnorm_b, l0_knorm_b, l0_w_out, l0_norm_ffn, l0_w_gate, l0_w_up, l0_w_down, l1_norm_mix, l1_w_in, l1_qnorm_c, l1_knorm_c, l1_lambda_q1, l1_lambda_k1, l1_lambda_q2, l1_lambda_k2, l1_subln, l1_w_out, l1_norm_ffn, l1_w_router, l1_e_gate, l1_e_up, l1_e_down):
    x = _layer0(x, l0_norm_mix, l0_w_in, l0_qnorm_a, l0_knorm_a, l0_qnorm_b, l0_knorm_b, l0_w_out,
                l0_norm_ffn, l0_w_gate, l0_w_up, l0_w_down)
    return _layer1(x, l1_norm_mix, l1_w_in, l1_qnorm_c, l1_knorm_c, l1_lambda_q1, l1_lambda_k1,
                   l1_lambda_q2, l1_lambda_k2, l1_subln, l1_w_out, l1_norm_ffn, l1_w_router,
                   l1_e_gate, l1_e_up, l1_e_down)
```

```python
import functools
import math

import jax
import jax.numpy as jnp
from jax import lax
from jax.experimental import pallas as pl
from jax.experimental.pallas import tpu as pltpu

F32 = jnp.float32
BF16 = jnp.bfloat16

HEAD_DIM = 128
LANES = 128
A_HEADS = 8
A_KV_HEADS = 2
B_HEADS = 8
B_PATTERNS = ((128, 1), (512, 4), (2048, 16))
C_HEADS = 8
N_EXPERTS = 8
TOP_K = 2
GRID_W = 64
ROPE_THETA = 10000.0
NORM_EPS = 1e-6
NEG_INF = -1e30
LOG2E = 1.4426950408889634
VMEM_LIMIT_BYTES = 56 * 1024 * 1024

_NT = (((1,), (1,)), ((), ()))


def _params(*sem):
    return pltpu.CompilerParams(dimension_semantics=sem, vmem_limit_bytes=VMEM_LIMIT_BYTES)


def _rms(x, eps=NORM_EPS):
    return x * lax.rsqrt(jnp.mean(x * x, axis=-1, keepdims=True) + eps)


PROJ_ROW_CHUNKS = 2


def _normproj_kernel(*refs, heads_per_tile, has_rope):
    if has_rope:
        x_ref, g_ref, w_ref, cols_ref, cos_ref, sa_ref, sb_ref, o_ref, xn_ref = refs
    else:
        x_ref, g_ref, w_ref, cols_ref, o_ref, xn_ref = refs

    @pl.when(pl.program_id(1) == 0)
    def _():
        xn_ref[...] = (_rms(x_ref[...]) * g_ref[...]).astype(BF16)

    rows = x_ref.shape[0] // PROJ_ROW_CHUNKS
    for r in range(PROJ_ROW_CHUNKS):
        rs = slice(r * rows, (r + 1) * rows)
        acc = jnp.dot(xn_ref[rs, :], w_ref[...], preferred_element_type=F32)
        for h in range(heads_per_tile):
            sl = slice(h * HEAD_DIM, (h + 1) * HEAD_DIM)
            y = acc[:, sl]
            inv = lax.rsqrt(jnp.mean(y * y, axis=-1, keepdims=True) + NORM_EPS)
            y = y * (inv * cols_ref[0:1, sl] + cols_ref[1:2, sl])
            if has_rope:
                rot = (y * cos_ref[rs, :] + pltpu.roll(y, 96, 1) * sa_ref[rs, :]
                       + pltpu.roll(y, 32, 1) * sb_ref[rs, :])
                y = y + cols_ref[2:3, sl] * (rot - y)
            o_ref[rs, sl] = (y * cols_ref[3:4, sl]).astype(o_ref.dtype)


def _normproj(x, g, w, head_ops, gains, rope_tables, *, col0, tm, tn):
    T, D = x.shape
    N = len(head_ops) * HEAD_DIM
    has_rope = rope_tables is not None
    zero, one = jnp.zeros((HEAD_DIM,), F32), jnp.ones((HEAD_DIM,), F32)
    cols = jnp.stack([
        jnp.concatenate([zero if gi is None else gains[gi] for gi, _, _ in head_ops]),
        jnp.concatenate([one if gi is None else zero for gi, _, _ in head_ops]),
        jnp.concatenate([one if rope else zero for _, rope, _ in head_ops]),
        jnp.concatenate([one * scale for _, _, scale in head_ops])])
    cols = jnp.concatenate([cols, jnp.zeros((4, N), F32)])
    jt0 = col0 // tn
    in_specs = [
        pl.BlockSpec((tm, D), lambda i, j: (i, 0)),
        pl.BlockSpec((1, D), lambda i, j: (0, 0)),
        pl.BlockSpec((D, tn), lambda i, j: (0, jt0 + j)),
        pl.BlockSpec((8, tn), lambda i, j: (0, j)),
    ]
    args = [x, g.reshape(1, D), w, cols]
    if has_rope:
        ns = rope_tables[0].shape[0] // tm
        for t in rope_tables:
            in_specs.append(pl.BlockSpec((tm, HEAD_DIM), lambda i, j: (i % ns, 0)))
            args.append(t)
    return pl.pallas_call(
        functools.partial(_normproj_kernel, heads_per_tile=tn // HEAD_DIM, has_rope=has_rope),
        grid=(T // tm, N // tn),
        in_specs=in_specs,
        out_specs=pl.BlockSpec((tm, tn), lambda i, j: (i, j)),
        out_shape=jax.ShapeDtypeStruct((T, N), BF16),
        scratch_shapes=[pltpu.VMEM((tm, D), BF16)],
        compiler_params=_params("parallel", "arbitrary"),
        name="normproj_rope" if has_rope else "normproj",
    )(*args)


def _rope_tables(S):
    rows = S // GRID_W
    pos = jnp.arange(S, dtype=jnp.int32)
    row = (pos // GRID_W).astype(F32)
    col = (pos % GRID_W).astype(F32)
    del rows
    nf = HEAD_DIM // 4
    inv = ROPE_THETA ** (-jnp.arange(nf, dtype=F32) / nf)
    ang_row = row[:, None] * inv
    ang_col = col[:, None] * inv
    ang = jnp.concatenate([ang_row, ang_row, ang_col, ang_col], axis=-1)
    cos = jnp.cos(ang)
    sin = jnp.sin(ang)
    quarter = (jnp.arange(HEAD_DIM) // nf) % 2
    sa = jnp.where(quarter[None, :] == 0, -sin, 0.0)
    sb = jnp.where(quarter[None, :] == 1, sin, 0.0)
    return cos, sa, sb


def _gqa_kernel(q_ref, k_ref, v_ref, o_ref, m_ref, l_ref, acc_ref, *, group, nk):
    j = pl.program_id(3)

    @pl.when(j == 0)
    def _():
        m_ref[...] = jnp.full(m_ref.shape, NEG_INF, F32)
        l_ref[...] = jnp.zeros(l_ref.shape, F32)
        acc_ref[...] = jnp.zeros(acc_ref.shape, F32)

    k = k_ref[...]
    v = v_ref[...]
    for h in range(group):
        q = q_ref[:, h * HEAD_DIM:(h + 1) * HEAD_DIM]
        s = lax.dot_general(q, k, _NT, preferred_element_type=F32)
        m_prev = m_ref[h]
        m_new = jnp.maximum(m_prev, jnp.max(s, axis=1, keepdims=True))
        alpha = jnp.exp2(m_prev - m_new)
        p = jnp.exp2(s - m_new[:, :1])
        l_ref[h] = alpha * l_ref[h] + jnp.sum(p, axis=1, keepdims=True)
        acc_ref[h] = acc_ref[h] * alpha + jnp.dot(p.astype(BF16), v, preferred_element_type=F32)
        m_ref[h] = m_new

    @pl.when(j == nk - 1)
    def _():
        for h in range(group):
            o_ref[:, h * HEAD_DIM:(h + 1) * HEAD_DIM] = (acc_ref[h] / l_ref[h]).astype(o_ref.dtype)


def _gqa_bounded_kernel(q_ref, k_ref, v_ref, o_ref, l_ref, acc_ref, *, group, nk):
    j = pl.program_id(3)

    @pl.when(j == 0)
    def _():
        l_ref[...] = jnp.zeros(l_ref.shape, F32)
        acc_ref[...] = jnp.zeros(acc_ref.shape, F32)

    k = k_ref[...]
    v = v_ref[...]
    for h in range(group):
        q = q_ref[:, h * HEAD_DIM:(h + 1) * HEAD_DIM]
        p = jnp.exp2(lax.dot_general(q, k, _NT, preferred_element_type=F32))
        l_ref[h] += jnp.sum(p, axis=1, keepdims=True)
        acc_ref[h] += jnp.dot(p.astype(BF16), v, preferred_element_type=F32)

    @pl.when(j == nk - 1)
    def _():
        for h in range(group):
            o_ref[:, h * HEAD_DIM:(h + 1) * HEAD_DIM] = (acc_ref[h] / l_ref[h]).astype(o_ref.dtype)


def _gqa_attention(proj, B, S, *, q_col0, k_col0, v_col0, tq, tk, bounded):
    group = A_HEADS // A_KV_HEADS
    nq, nk = S // tq, S // tk
    gw = group * HEAD_DIM
    if bounded:
        body = functools.partial(_gqa_bounded_kernel, group=group, nk=nk)
        scratch = [pltpu.VMEM((group, tq, LANES), F32), pltpu.VMEM((group, tq, HEAD_DIM), F32)]
    else:
        body = functools.partial(_gqa_kernel, group=group, nk=nk)
        scratch = [pltpu.VMEM((group, tq, LANES), F32), pltpu.VMEM((group, tq, LANES), F32),
                   pltpu.VMEM((group, tq, HEAD_DIM), F32)]
    return pl.pallas_call(
        body,
        grid=(B, A_KV_HEADS, nq, nk),
        in_specs=[
            pl.BlockSpec((tq, gw), lambda b, g, i, j: (b * nq + i, q_col0 // gw + g)),
            pl.BlockSpec((tk, HEAD_DIM), lambda b, g, i, j: (b * nk + j, k_col0 // HEAD_DIM + g)),
            pl.BlockSpec((tk, HEAD_DIM), lambda b, g, i, j: (b * nk + j, v_col0 // HEAD_DIM + g)),
        ],
        out_specs=pl.BlockSpec((tq, gw), lambda b, g, i, j: (b * nq + i, g)),
        out_shape=jax.ShapeDtypeStruct((B * S, A_HEADS * HEAD_DIM), BF16),
        scratch_shapes=scratch,
        compiler_params=_params("parallel", "parallel", "parallel", "arbitrary"),
        name="gqa_bounded" if bounded else "gqa_attention",
    )(proj, proj, proj)


def _dilated_kernel(q_ref, *refs, nside, nq):
    nb = 2 * nside + 1
    k_refs, v_refs = refs[:nb], refs[nb:2 * nb]
    tbl_ref, o_ref = refs[2 * nb], refs[2 * nb + 1]
    i = pl.program_id(2)
    q = q_ref[...]
    scores = []
    m = None
    for d in range(nb):
        blk = i + (d - nside)
        in_range = (blk >= 0) & (blk < nq)
        s = lax.dot_general(q, k_refs[d][...], _NT, preferred_element_type=F32) + tbl_ref[0, d]
        s = jnp.where(in_range, s, NEG_INF)
        scores.append(s)
        md = jnp.max(s, axis=1, keepdims=True)
        m = md if m is None else jnp.maximum(m, md)
    l = None
    acc = None
    for d in range(nb):
        p = jnp.exp2(scores[d] - m)
        ld = jnp.sum(p, axis=1, keepdims=True)
        ad = jnp.dot(p.astype(BF16), v_refs[d][...], preferred_element_type=F32)
        l = ld if l is None else l + ld
        acc = ad if acc is None else acc + ad
    o_ref[...] = (acc / l).astype(o_ref.dtype)


def _dilated_bounded_kernel(q_ref, *refs, nside, nq):
    nb = 2 * nside + 1
    k_refs, v_refs = refs[:nb], refs[nb:2 * nb]
    tbl_ref, o_ref = refs[2 * nb], refs[2 * nb + 1]
    i = pl.program_id(2)
    q = q_ref[...]
    l = None
    acc = None
    for d in range(nb):
        blk = i + (d - nside)
        in_range = (blk >= 0) & (blk < nq)
        s = lax.dot_general(q, k_refs[d][...], _NT, preferred_element_type=F32) + tbl_ref[0, d]
        p = jnp.exp2(jnp.where(in_range, s, NEG_INF))
        ld = jnp.sum(p, axis=1, keepdims=True)
        ad = jnp.dot(p.astype(BF16), v_refs[d][...], preferred_element_type=F32)
        l = ld if l is None else l + ld
        acc = ad if acc is None else acc + ad
    o_ref[...] = (acc / l).astype(o_ref.dtype)


def _dilated_bias_table(slopes, tq, nside):
    nb = 2 * nside + 1
    a = jnp.arange(tq, dtype=jnp.int32)
    d = ((jnp.arange(nb, dtype=jnp.int32)[:, None, None] - nside) * tq
         + a[None, None, :] - a[None, :, None])
    ad = jnp.abs(d)
    count = jnp.zeros(d.shape, F32)
    for window, dil in B_PATTERNS:
        reach = (window // (2 * dil)) * dil
        count = count + ((ad % dil == 0) & (ad <= reach)).astype(F32)
    logc = jnp.where(count > 0, jnp.log2(jnp.maximum(count, 1.0)), NEG_INF)
    bias = -(slopes * LOG2E)[:, None, None, None] * ad.astype(F32)[None]
    return jnp.where(count[None] > 0, bias + logc[None], NEG_INF)


def _dilated_attention(proj, B, S, slopes, *, q_col0, k_col0, v_col0, tq, bounded):
    reach = max((w // (2 * dl)) * dl for w, dl in B_PATTERNS)
    nside = -(-reach // tq)
    nb = 2 * nside + 1
    nq = S // tq
    table = _dilated_bias_table(slopes, tq, nside)

    def kv_spec(col0, d):
        def imap(h, b, i):
            return (b * nq + jnp.clip(i + (d - nside), 0, nq - 1), col0 // HEAD_DIM + h)
        return pl.BlockSpec((tq, HEAD_DIM), imap)

    in_specs = [pl.BlockSpec((tq, HEAD_DIM), lambda h, b, i: (b * nq + i, q_col0 // HEAD_DIM + h))]
    in_specs += [kv_spec(k_col0, d) for d in range(nb)]
    in_specs += [kv_spec(v_col0, d) for d in range(nb)]
    in_specs += [pl.BlockSpec((1, nb, tq, tq), lambda h, b, i: (h, 0, 0, 0))]
    return pl.pallas_call(
        functools.partial(_dilated_bounded_kernel if bounded else _dilated_kernel,
                          nside=nside, nq=nq),
        grid=(B_HEADS, B, nq),
        in_specs=in_specs,
        out_specs=pl.BlockSpec((tq, HEAD_DIM), lambda h, b, i: (b * nq + i, h)),
        out_shape=jax.ShapeDtypeStruct((B * S, B_HEADS * HEAD_DIM), BF16),
        compiler_params=_params("parallel", "parallel", "parallel"),
        name="dilated_bounded" if bounded else "dilated_attention",
    )(*([proj] * (1 + 2 * nb)), table)


def _outproj_kernel(*refs, n_parts):
    a_refs = refs[:n_parts]
    w_refs = refs[n_parts:2 * n_parts]
    r_ref, o_ref = refs[2 * n_parts], refs[2 * n_parts + 1]
    acc = r_ref[...]
    for a_ref, w_ref in zip(a_refs, w_refs):
        acc = acc + jnp.dot(a_ref[...], w_ref[...], preferred_element_type=F32)
    o_ref[...] = acc


def _outproj_residual(parts, w, res, *, tm, tn):
    T, N = res.shape
    n_parts = len(parts)
    in_specs, w_args, off = [], [], 0
    for a in parts:
        kp = a.shape[1]
        in_specs.append(pl.BlockSpec((tm, kp), lambda i, j: (i, 0)))
        w_args.append((kp, off // kp))
        off += kp
    for kp, blk in w_args:
        in_specs.append(pl.BlockSpec((kp, tn), lambda i, j, blk=blk: (blk, j)))
    in_specs.append(pl.BlockSpec((tm, tn), lambda i, j: (i, j)))
    return pl.pallas_call(
        functools.partial(_outproj_kernel, n_parts=n_parts),
        grid=(T // tm, N // tn),
        in_specs=in_specs,
        out_specs=pl.BlockSpec((tm, tn), lambda i, j: (i, j)),
        out_shape=jax.ShapeDtypeStruct((T, N), F32),
        compiler_params=_params("parallel", "arbitrary"),
        name="outproj_residual",
    )(*parts, *([w] * n_parts), res)


def _silu(g):
    return g / (1.0 + jnp.exp(-g))


def _ffn_kernel(x_ref, g_ref, wg_ref, wu_ref, wd_ref, o_ref, xn_ref):
    @pl.when(pl.program_id(1) == 0)
    def _():
        x = x_ref[...]
        xn_ref[...] = (_rms(x) * g_ref[...]).astype(BF16)
        o_ref[...] = x

    xn = xn_ref[...]
    gate = jnp.dot(xn, wg_ref[...], preferred_element_type=F32)
    up = jnp.dot(xn, wu_ref[...], preferred_element_type=F32)
    hid = (_silu(gate) * up).astype(BF16)
    o_ref[...] += jnp.dot(hid, wd_ref[...], preferred_element_type=F32)


def _ffn_residual(x, g, w_gate, w_up, w_down, *, tm, tf):
    T, D = x.shape
    Fd = w_gate.shape[1]
    nf = Fd // tf
    return pl.pallas_call(
        _ffn_kernel,
        grid=(T // tm, nf),
        in_specs=[
            pl.BlockSpec((tm, D), lambda i, f: (i, 0)),
            pl.BlockSpec((1, D), lambda i, f: (0, 0)),
            pl.BlockSpec((D, tf), lambda i, f: (0, f)),
            pl.BlockSpec((D, tf), lambda i, f: (0, f)),
            pl.BlockSpec((tf, D), lambda i, f: (f, 0)),
        ],
        out_specs=pl.BlockSpec((tm, D), lambda i, f: (i, 0)),
        out_shape=jax.ShapeDtypeStruct((T, D), F32),
        scratch_shapes=[pltpu.VMEM((tm, D), BF16)],
        compiler_params=_params("parallel", "arbitrary"),
        name="ffn_residual",
    )(x, g.reshape(1, D), w_gate, w_up, w_down)


def _diff_kernel(slopes_ref, q_ref, k_ref, v_ref, dmat_ref, lq1_ref, lk1_ref, lq2_ref, lk2_ref,
                 subln_ref, o_ref, m_ref, l_ref, acc_ref, *, tq, tk, nk, lambda_init):
    h = pl.program_id(1)
    i = pl.program_id(2)
    j = pl.program_id(3)

    @pl.when(j == 0)
    def _():
        m_ref[...] = jnp.full(m_ref.shape, NEG_INF, F32)
        l_ref[...] = jnp.zeros(l_ref.shape, F32)
        acc_ref[...] = jnp.zeros(acc_ref.shape, F32)

    neg_slope = -slopes_ref[h] * LOG2E
    bias = _alibi_bias(dmat_ref[...], (j * tk - i * tq).astype(F32), neg_slope, tk)
    v = v_ref[...]
    for c in range(2):
        sl = slice(c * HEAD_DIM, (c + 1) * HEAD_DIM)
        s = lax.dot_general(q_ref[:, sl], k_ref[:, sl], _NT, preferred_element_type=F32) + bias
        m_prev = m_ref[c]
        m_new = jnp.maximum(m_prev, jnp.max(s, axis=1, keepdims=True))
        alpha = jnp.exp2(m_prev - m_new)
        p = jnp.exp2(s - m_new[:, :1])
        l_ref[c] = alpha * l_ref[c] + jnp.sum(p, axis=1, keepdims=True)
        acc_ref[c] = acc_ref[c] * alpha[:, :1] + jnp.dot(p.astype(BF16), v,
                                                         preferred_element_type=F32)
        m_ref[c] = m_new

    @pl.when(j == nk - 1)
    def _():
        _diff_finalize(acc_ref, l_ref, lq1_ref, lk1_ref, lq2_ref, lk2_ref, subln_ref, o_ref,
                       lambda_init)


def _alibi_bias(dbase, offset, neg_slope, tk):
    return jnp.concatenate([neg_slope * jnp.abs(dbase + (offset + float(cb * LANES)))
                            for cb in range(tk // LANES)], axis=1)


def _diff_finalize(acc_ref, l_ref, lq1_ref, lk1_ref, lq2_ref, lk2_ref, subln_ref, o_ref, lambda_init):
    lam = (jnp.exp(jnp.sum(lq1_ref[...] * lk1_ref[...], axis=-1, keepdims=True))
           - jnp.exp(jnp.sum(lq2_ref[...] * lk2_ref[...], axis=-1, keepdims=True))
           + lambda_init)
    o = acc_ref[0] / l_ref[0][:, :1] - lam * (acc_ref[1] / l_ref[1][:, :1])
    o = _rms(o) * subln_ref[...] * (1.0 - lambda_init)
    o_ref[...] = o.astype(o_ref.dtype)


def _diff_bounded_kernel(slopes_ref, q_ref, k_ref, v_ref, dmat_ref, lq1_ref, lk1_ref, lq2_ref,
                         lk2_ref, subln_ref, o_ref, l_ref, acc_ref, *, tq, tk, nk, lambda_init):
    h = pl.program_id(1)
    i = pl.program_id(2)
    j = pl.program_id(3)

    @pl.when(j == 0)
    def _():
        l_ref[...] = jnp.zeros(l_ref.shape, F32)
        acc_ref[...] = jnp.zeros(acc_ref.shape, F32)

    neg_slope = -slopes_ref[h] * LOG2E
    q0 = i * tq
    k0 = j * tk
    v = v_ref[...]

    def scores(c):
        sl = slice(c * HEAD_DIM, (c + 1) * HEAD_DIM)
        return lax.dot_general(q_ref[:, sl], k_ref[:, sl], _NT, preferred_element_type=F32)

    overlaps = (k0 < q0 + tq) & (q0 < k0 + tk)

    @pl.when(overlaps)
    def _():
        for c in range(2):
            p = jnp.exp2(scores(c) + _alibi_bias(dmat_ref[...], (k0 - q0).astype(F32), neg_slope, tk))
            l_ref[c] += jnp.sum(p, axis=1, keepdims=True)
            acc_ref[c] += jnp.dot(p.astype(BF16), v, preferred_element_type=F32)

    @pl.when(jnp.logical_not(overlaps))
    def _():
        right = k0 >= q0 + tq
        gap = jnp.where(right, k0 - (q0 + tq), q0 - (k0 + tk)).astype(F32)
        key = lax.broadcasted_iota(jnp.int32, (1, tk), 1)
        qry = lax.broadcasted_iota(jnp.int32, (tq, 1), 0)
        key_bias = neg_slope * jnp.where(right, key, tk - key).astype(F32)
        qry_fac = jnp.exp2(neg_slope * (jnp.where(right, tq - qry, qry).astype(F32) + gap))
        for c in range(2):
            p = jnp.exp2(scores(c) + key_bias)
            l_ref[c] += qry_fac * jnp.sum(p, axis=1, keepdims=True)
            acc_ref[c] += qry_fac * jnp.dot(p.astype(BF16), v, preferred_element_type=F32)

    @pl.when(j == nk - 1)
    def _():
        _diff_finalize(acc_ref, l_ref, lq1_ref, lk1_ref, lq2_ref, lk2_ref, subln_ref, o_ref,
                       lambda_init)


def _diff_attention(proj, B, S, slopes, lam_vecs, subln, lambda_init, *, tq, tk, bounded):
    H = C_HEADS
    hw = 2 * HEAD_DIM
    nq, nk = S // tq, S // tk
    dmat = (jnp.arange(LANES, dtype=F32)[None, :] - jnp.arange(tq, dtype=F32)[:, None])
    stat = [pltpu.VMEM((2, tq, LANES), F32)]
    if bounded:
        body, name = _diff_bounded_kernel, "diff_bounded"
    else:
        body, name, stat = _diff_kernel, "diff_attention", stat * 2
    vec_spec = pl.BlockSpec((1, HEAD_DIM), lambda b, h, i, j, s: (0, 0))
    grid_spec = pltpu.PrefetchScalarGridSpec(
        num_scalar_prefetch=1,
        grid=(B, H, nq, nk),
        in_specs=[
            pl.BlockSpec((tq, hw), lambda b, h, i, j, s: (b * nq + i, h)),
            pl.BlockSpec((tk, hw), lambda b, h, i, j, s: (b * nk + j, H + h)),
            pl.BlockSpec((tk, hw), lambda b, h, i, j, s: (b * nk + j, 2 * H + h)),
            pl.BlockSpec((tq, LANES), lambda b, h, i, j, s: (0, 0)),
            vec_spec, vec_spec, vec_spec, vec_spec,
            pl.BlockSpec((1, hw), lambda b, h, i, j, s: (0, 0)),
        ],
        out_specs=pl.BlockSpec((tq, hw), lambda b, h, i, j, s: (b * nq + i, h)),
        scratch_shapes=stat + [pltpu.VMEM((2, tq, hw), F32)],
    )
    return pl.pallas_call(
        functools.partial(body, tq=tq, tk=tk, nk=nk, lambda_init=lambda_init),
        grid_spec=grid_spec,
        out_shape=jax.ShapeDtypeStruct((B * S, H * hw), BF16),
        compiler_params=_params("parallel", "parallel", "parallel", "arbitrary"),
        name=name,
    )(slopes, proj, proj, proj, dmat, *[v.reshape(1, HEAD_DIM) for v in lam_vecs],
      subln.reshape(1, hw))


def _router_kernel(x_ref, g_ref, wr_ref, meta_ref, cnt_ref, base_ref, *, tm, n_exp):
    i = pl.program_id(0)

    @pl.when(i == 0)
    def _():
        base_ref[...] = jnp.zeros(base_ref.shape, F32)

    hn = _rms(x_ref[...]) * g_ref[...]
    logits = jnp.dot(hn, wr_ref[...], preferred_element_type=F32,
                     precision=lax.Precision.HIGHEST)
    lane = lax.broadcasted_iota(jnp.int32, (tm, LANES), 1)
    logits = jnp.where(lane < n_exp, logits, NEG_INF)
    t1 = jnp.max(logits, axis=1, keepdims=True)
    i1 = jnp.min(jnp.where(logits == t1, lane, LANES), axis=1, keepdims=True)
    rest = jnp.where(lane == i1, NEG_INF, logits)
    t2 = jnp.max(rest, axis=1, keepdims=True)
    i2 = jnp.min(jnp.where(rest == t2, lane, LANES), axis=1, keepdims=True)
    ex = jnp.exp(t2 - t1)
    g1 = 1.0 / (1.0 + ex)
    g2 = ex * g1
    oh1 = lane == i1
    oh2 = lane == i2
    member = jnp.where(oh1 | oh2, 1.0, 0.0)
    row = lax.broadcasted_iota(jnp.int32, (tm, tm), 0)
    col = lax.broadcasted_iota(jnp.int32, (tm, tm), 1)
    strict_lower = jnp.where(col < row, 1.0, 0.0).astype(BF16)
    before = jnp.dot(strict_lower, member.astype(BF16), preferred_element_type=F32) + base_ref[...]
    r1 = jnp.sum(jnp.where(oh1, before, 0.0), axis=1, keepdims=True)
    r2 = jnp.sum(jnp.where(oh2, before, 0.0), axis=1, keepdims=True)
    base_ref[...] += jnp.sum(member, axis=0, keepdims=True)
    meta = jnp.where(lane == 0, i1.astype(F32), 0.0)
    meta = jnp.where(lane == 1, i2.astype(F32), meta)
    meta = jnp.where(lane == 2, g1, meta)
    meta = jnp.where(lane == 3, g2, meta)
    meta = jnp.where(lane == 4, r1, meta)
    meta = jnp.where(lane == 5, r2, meta)
    meta_ref[...] = meta
    cnt_ref[...] = base_ref[...]


def _router(x, g, w_router, *, tm):
    T, D = x.shape
    n_exp = w_router.shape[1]
    wr = jnp.zeros((D, LANES), F32).at[:, :n_exp].set(w_router)
    return pl.pallas_call(
        functools.partial(_router_kernel, tm=tm, n_exp=n_exp),
        grid=(T // tm,),
        in_specs=[
            pl.BlockSpec((tm, D), lambda i: (i, 0)),
            pl.BlockSpec((1, D), lambda i: (0, 0)),
            pl.BlockSpec((D, LANES), lambda i: (0, 0)),
        ],
        out_specs=[pl.BlockSpec((tm, LANES), lambda i: (i, 0)),
                   pl.BlockSpec((1, LANES), lambda i: (0, 0))],
        out_shape=[jax.ShapeDtypeStruct((T, LANES), F32),
                   jax.ShapeDtypeStruct((1, LANES), F32)],
        scratch_shapes=[pltpu.VMEM((1, LANES), F32)],
        compiler_params=_params("arbitrary"),
        name="moe_router",
    )(x, g.reshape(1, D), wr)


def _row_copy(src_ref, src_row, dst_ref, dst_row, sem):
    return pltpu.make_async_copy(src_ref.at[pl.ds(src_row, 1)], dst_ref.at[pl.ds(dst_row, 1)], sem)


def _dispatch_kernel(pend_ref, dest_hbm, x_ref, xs_hbm, idx_ref, zero_ref, idx_sem, row_sem, zero_sem,
                     *, tm, bm, n_rows):
    i = pl.program_id(0)

    @pl.when(i == 0)
    def _():
        zero_ref[...] = jnp.zeros(zero_ref.shape, F32)
        fills = [pltpu.make_async_copy(zero_ref, xs_hbm.at[pl.ds(n_rows - (e + 1) * bm, bm)], zero_sem)
                 for e in range(N_EXPERTS)]
        for e in range(N_EXPERTS):
            start = pl.multiple_of(jnp.maximum(pend_ref[e] - bm, 0), 8)
            fills.append(pltpu.make_async_copy(zero_ref, xs_hbm.at[pl.ds(start, bm)], zero_sem))
        for f in fills:
            f.start()
            f.wait()

    n_idx = TOP_K * tm
    idx_copy = pltpu.make_async_copy(dest_hbm.at[pl.ds(i * n_idx, n_idx)], idx_ref, idx_sem)
    idx_copy.start()
    idx_copy.wait()

    def issue(t, carry):
        for k in range(TOP_K):
            _row_copy(x_ref, t, xs_hbm, idx_ref[TOP_K * t + k], row_sem).start()
        return carry

    lax.fori_loop(0, tm, issue, 0)

    for k in range(TOP_K):
        pltpu.make_async_copy(x_ref, xs_hbm.at[pl.ds(0, tm)], row_sem).wait()


def _dispatch(dest, x, pend, n_rows, *, tm, bm):
    T, D = x.shape
    grid_spec = pltpu.PrefetchScalarGridSpec(
        num_scalar_prefetch=1,
        grid=(T // tm,),
        in_specs=[pl.BlockSpec(memory_space=pl.ANY),
                  pl.BlockSpec((tm, D), lambda i, pend: (i, 0))],
        out_specs=pl.BlockSpec(memory_space=pl.ANY),
        scratch_shapes=[pltpu.SMEM((TOP_K * tm,), jnp.int32), pltpu.VMEM((bm, D), F32),
                        pltpu.SemaphoreType.DMA(()), pltpu.SemaphoreType.DMA(()),
                        pltpu.SemaphoreType.DMA(())],
    )
    return pl.pallas_call(
        functools.partial(_dispatch_kernel, tm=tm, bm=bm, n_rows=n_rows),
        grid_spec=grid_spec,
        out_shape=jax.ShapeDtypeStruct((n_rows, D), F32),
        compiler_params=pltpu.CompilerParams(dimension_semantics=("arbitrary",),
                                             vmem_limit_bytes=VMEM_LIMIT_BYTES,
                                             has_side_effects=True),
        name="moe_dispatch",
    )(pend.astype(jnp.int32), dest, x)


def _expert_kernel(be_ref, bn_ref, x_ref, g_ref, wg_ref, wu_ref, wd_ref, y_ref, xn_ref, *, bm):
    del be_ref
    n_valid = bn_ref[pl.program_id(0)]

    @pl.when(pl.program_id(1) == 0)
    def _():
        rows = lax.broadcasted_iota(jnp.int32, (bm, 1), 0)
        x = jnp.where(rows < n_valid, x_ref[...], 0.0)
        xn_ref[...] = (_rms(x) * g_ref[...]).astype(BF16)
        y_ref[...] = jnp.zeros(y_ref.shape, F32)

    def ffn_rows(rs):
        xn = xn_ref[rs, :]
        gate = jnp.dot(xn, wg_ref[...], preferred_element_type=F32)
        up = jnp.dot(xn, wu_ref[...], preferred_element_type=F32)
        hid = (_silu(gate) * up).astype(BF16)
        y_ref[rs, :] += jnp.dot(hid, wd_ref[...], preferred_element_type=F32)

    half = bm // 2

    @pl.when(n_valid > half)
    def _():
        ffn_rows(slice(0, bm))

    @pl.when((n_valid > 0) & (n_valid <= half))
    def _():
        ffn_rows(slice(0, half))


def _expert_ffn(xs, g, e_gate, e_up, e_down, blk_expert, blk_valid, *, bm, tf):
    P, D = xs.shape
    Fd = e_gate.shape[2]
    nf = Fd // tf
    nblk = P // bm

    def f_idx(b, f, bn):
        return jnp.where(bn[b] > 0, f, nf - 1)

    grid_spec = pltpu.PrefetchScalarGridSpec(
        num_scalar_prefetch=2,
        grid=(nblk, nf),
        in_specs=[
            pl.BlockSpec((bm, D), lambda b, f, be, bn: (b, 0)),
            pl.BlockSpec((1, D), lambda b, f, be, bn: (0, 0)),
            pl.BlockSpec((None, D, tf), lambda b, f, be, bn: (be[b], 0, f_idx(b, f, bn))),
            pl.BlockSpec((None, D, tf), lambda b, f, be, bn: (be[b], 0, f_idx(b, f, bn))),
            pl.BlockSpec((None, tf, D), lambda b, f, be, bn: (be[b], f_idx(b, f, bn), 0)),
        ],
        out_specs=pl.BlockSpec((bm, D), lambda b, f, be, bn: (b, 0)),
        scratch_shapes=[pltpu.VMEM((bm, D), BF16)],
    )
    return pl.pallas_call(
        functools.partial(_expert_kernel, bm=bm),
        grid_spec=grid_spec,
        out_shape=jax.ShapeDtypeStruct((P, D), F32),
        compiler_params=_params("arbitrary", "arbitrary"),
        name="moe_experts",
    )(blk_expert, blk_valid, xs, g.reshape(1, D), e_gate, e_up, e_down)


def _combine_kernel(dest_hbm, ys_hbm, x_ref, meta_ref, o_ref, idx_ref, ybuf_ref, idx_sem, row_sem,
                    *, tm):
    i = pl.program_id(0)
    n_idx = TOP_K * tm
    idx_copy = pltpu.make_async_copy(dest_hbm.at[pl.ds(i * n_idx, n_idx)], idx_ref, idx_sem)
    idx_copy.start()
    idx_copy.wait()

    def issue(t, carry):
        for k in range(TOP_K):
            _row_copy(ys_hbm, idx_ref[TOP_K * t + k], ybuf_ref.at[k], t, row_sem).start()
        return carry

    lax.fori_loop(0, tm, issue, 0)

    for k in range(TOP_K):
        pltpu.make_async_copy(ys_hbm.at[pl.ds(0, tm)], ybuf_ref.at[k], row_sem).wait()

    meta = meta_ref[...]
    out = x_ref[...]
    for k in range(TOP_K):
        out = out + meta[:, 2 + k:3 + k] * ybuf_ref[k]
    o_ref[...] = out


def _combine(dest, ys, x, meta, *, tm):
    T, D = x.shape
    return pl.pallas_call(
        functools.partial(_combine_kernel, tm=tm),
        grid=(T // tm,),
        in_specs=[pl.BlockSpec(memory_space=pl.ANY), pl.BlockSpec(memory_space=pl.ANY),
                  pl.BlockSpec((tm, D), lambda i: (i, 0)),
                  pl.BlockSpec((tm, LANES), lambda i: (i, 0))],
        out_specs=pl.BlockSpec((tm, D), lambda i: (i, 0)),
        out_shape=jax.ShapeDtypeStruct((T, D), F32),
        scratch_shapes=[pltpu.SMEM((TOP_K * tm,), jnp.int32),
                        pltpu.VMEM((TOP_K, tm, D), F32),
                        pltpu.SemaphoreType.DMA(()), pltpu.SemaphoreType.DMA(())],
        compiler_params=_params("arbitrary"),
        name="moe_combine",
    )(dest, ys, x, meta)


def _moe_residual(x, g, w_router, e_gate, e_up, e_down, *, tm_route, tm_move, bm, tf):
    T, D = x.shape
    meta, counts = _router(x, g, w_router, tm=tm_route)
    counts = counts[0, :N_EXPERTS].astype(jnp.int32)
    padded = (counts + bm - 1) // bm * bm
    pend = jnp.cumsum(padded)
    pstart = pend - padded
    expert = meta[:, 0:TOP_K].astype(jnp.int32)
    rank = meta[:, 4:4 + TOP_K].astype(jnp.int32)
    dest = (pstart[expert] + rank).reshape(T * TOP_K)
    n_rows = T * TOP_K + N_EXPERTS * bm
    nblk = n_rows // bm
    blk_row0 = jnp.arange(nblk, dtype=jnp.int32) * bm
    blk_expert = jnp.minimum(jnp.searchsorted(pend, blk_row0, side='right'),
                             N_EXPERTS - 1).astype(jnp.int32)
    blk_valid = jnp.clip(counts[blk_expert] - (blk_row0 - pstart[blk_expert]), 0, bm)
    blk_valid = jnp.where(blk_row0 < pend[-1], blk_valid, 0).astype(jnp.int32)
    last_used = jnp.maximum(pend[-1] // bm - 1, 0)
    blk_expert = jnp.where(blk_row0 < pend[-1], blk_expert, blk_expert[last_used])

    xs = _dispatch(dest, x, pend, n_rows, tm=tm_move, bm=bm)
    ys = _expert_ffn(xs, g, e_gate, e_up, e_down, blk_expert, blk_valid, bm=bm, tf=tf)
    return _combine(dest, ys, x, meta, tm=tm_move)


def _alibi_slopes(n_heads):
    return 2.0 ** (-8.0 * jnp.arange(1, n_heads + 1, dtype=F32) / n_heads)


def _tile(n, want):
    t = min(n, want)
    while n % t:
        t //= 2
    return t


Q_SCALE = HEAD_DIM ** -0.5 * LOG2E
PROJ_TN = 512


BOUNDED_SCORE_LIMIT = 60.0


def _score_bound(q_gain, k_gain):
    return 1.01 * HEAD_DIM * Q_SCALE * jnp.max(jnp.abs(q_gain)) * jnp.max(jnp.abs(k_gain))


def _layer0(x, norm_mix, w_in, qn_a, kn_a, qn_b, kn_b, w_out, norm_ffn, w_gate, w_up, w_down):
    B, S, D = x.shape
    T = B * S
    xf = x.reshape(T, D)
    tm = _tile(S, 1024)
    a_q, a_kv, b_dim = A_HEADS * HEAD_DIM, A_KV_HEADS * HEAD_DIM, B_HEADS * HEAD_DIM
    w_in = w_in.astype(BF16)
    ops_a = ([(0, True, Q_SCALE)] * A_HEADS + [(1, True, 1.0)] * A_KV_HEADS
             + [(None, False, 1.0)] * A_KV_HEADS)
    ops_b = [(0, False, Q_SCALE)] * B_HEADS + [(1, False, 1.0)] * B_HEADS + [(None, False, 1.0)] * B_HEADS
    proj_a = _normproj(xf, norm_mix, w_in, ops_a, [qn_a, kn_a], _rope_tables(S), col0=0,
                       tm=tm, tn=PROJ_TN)
    proj_b = _normproj(xf, norm_mix, w_in, ops_b, [qn_b, kn_b], None, col0=a_q + 2 * a_kv,
                       tm=tm, tn=PROJ_TN)
    gqa = functools.partial(_gqa_attention, B=B, S=S, q_col0=0, k_col0=a_q, v_col0=a_q + a_kv)
    o_a = lax.cond(_score_bound(qn_a, kn_a) <= BOUNDED_SCORE_LIMIT,
                   lambda p: gqa(p, tq=_tile(S, 1024), tk=_tile(S, 4096), bounded=True),
                   lambda p: gqa(p, tq=_tile(S, 512), tk=_tile(S, 1024), bounded=False), proj_a)
    dil = functools.partial(_dilated_attention, B=B, S=S, slopes=_alibi_slopes(B_HEADS), q_col0=0,
                            k_col0=b_dim, v_col0=2 * b_dim, tq=_tile(S, 512))
    o_b = lax.cond(_score_bound(qn_b, kn_b) <= BOUNDED_SCORE_LIMIT,
                   lambda p: dil(p, bounded=True), lambda p: dil(p, bounded=False), proj_b)
    x1 = _outproj_residual([o_a, o_b], w_out.astype(BF16), xf, tm=tm, tn=512)
    x2 = _ffn_residual(x1, norm_ffn, w_gate.astype(BF16), w_up.astype(BF16), w_down.astype(BF16),
                       tm=_tile(T, 512), tf=512)
    return x2.reshape(B, S, D)


def _layer1(x, norm_mix, w_in, qn_c, kn_c, lam_q1, lam_k1, lam_q2, lam_k2, subln, w_out, norm_ffn,
            w_router, e_gate, e_up, e_down):
    B, S, D = x.shape
    T = B * S
    xf = x.reshape(T, D)
    tm = _tile(S, 1024)
    head_ops = ([(0, False, Q_SCALE)] * (2 * C_HEADS) + [(1, False, 1.0)] * (2 * C_HEADS)
                + [(None, False, 1.0)] * (2 * C_HEADS))
    proj = _normproj(xf, norm_mix, w_in.astype(BF16), head_ops, [qn_c, kn_c], None, col0=0,
                     tm=tm, tn=2 * PROJ_TN)
    lambda_init = 0.8 - 0.6 * math.exp(-0.3 * 1)
    diff = functools.partial(_diff_attention, B=B, S=S, slopes=_alibi_slopes(C_HEADS),
                             lam_vecs=(lam_q1, lam_k1, lam_q2, lam_k2), subln=subln,
                             lambda_init=lambda_init)
    o_c = lax.cond(_score_bound(qn_c, kn_c) <= BOUNDED_SCORE_LIMIT,
                   lambda p: diff(p, tq=_tile(S, 1024), tk=_tile(S, 4096), bounded=True),
                   lambda p: diff(p, tq=_tile(S, 512), tk=_tile(S, 1024), bounded=False), proj)
    x3 = _outproj_residual([o_c], w_out.astype(BF16), xf, tm=tm, tn=512)
    out = _moe_residual(x3, norm_ffn, w_router, e_gate.astype(BF16), e_up.astype(BF16),
                        e_down.astype(BF16), tm_route=_tile(T, 512), tm_move=_tile(T, 512),
                        bm=_tile(T, 512), tf=_tile(e_gate.shape[2], 1024))
    return out.reshape(B, S, D)


def kernel(x, l0_norm_mix,l0_w_in, l0_qnorm_a, l0_knorm_a, l0_qnorm_b, l0_knorm_b, l0_w_out, l0_norm_ffn, l0_w_gate, l0_w_up, l0_w_down, l1_norm_mix, l1_w_in, l1_qnorm_c, l1_knorm_c, l1_lambda_q1, l1_lambda_k1, l1_lambda_q2, l1_lambda_k2, l1_subln, l1_w_out, l1_norm_ffn, l1_w_router, l1_e_gate, l1_e_up, l1_e_down):
    x = _layer0(x, l0_norm_mix, l0_w_in, l0_qnorm_a, l0_knorm_a, l0_qnorm_b, l0_knorm_b, l0_w_out,
                l0_norm_ffn, l0_w_gate, l0_w_up, l0_w_down)
    return _layer1(x, l1_norm_mix, l1_w_in, l1_qnorm_c, l1_knorm_c, l1_lambda_q1, l1_lambda_k1,
                   l1_lambda_q2, l1_lambda_k2, l1_subln, l1_w_out, l1_norm_ffn, l1_w_router,
                   l1_e_gate, l1_e_up, l1_e_down)
```

```python
import functools
import math
from typing import NamedTuple

import jax
import jax.numpy as jnp
from jax import lax
from jax.experimental import pallas as pl
from jax.experimental.pallas import tpu as pltpu

F32 = jnp.float32
BF16 = jnp.bfloat16

HEAD_DIM = 128
LANES = 128
A_HEADS = 8
A_KV_HEADS = 2
B_HEADS = 8
B_PATTERNS = ((128, 1), (512, 4), (2048, 16))
C_HEADS = 8
N_EXPERTS = 8
TOP_K = 2
GRID_W = 64
ROPE_THETA = 10000.0
NORM_EPS = 1e-6
NEG_INF = -1e30
LOG2E = 1.4426950408889634
VMEM_LIMIT_BYTES = 56 * 1024 * 1024

_NT = (((1,), (1,)), ((), ()))


def _params(*sem):
    return pltpu.CompilerParams(dimension_semantics=sem, vmem_limit_bytes=VMEM_LIMIT_BYTES)


def _rms(x, eps=NORM_EPS):
    return x * lax.rsqrt(jnp.mean(x * x, axis=-1, keepdims=True) + eps)


PROJ_ROW_CHUNKS = 2


def _normproj_kernel(*refs, heads_per_tile, has_rope):
    if has_rope:
        x_ref, g_ref, w_ref, cols_ref, cos_ref, sa_ref, sb_ref, o_ref, xn_ref = refs
    else:
        x_ref, g_ref, w_ref, cols_ref, o_ref, xn_ref = refs

    @pl.when(pl.program_id(1) == 0)
    def _():
        xn_ref[...] = (_rms(x_ref[...]) * g_ref[...]).astype(BF16)

    rows = x_ref.shape[0] // PROJ_ROW_CHUNKS
    for r in range(PROJ_ROW_CHUNKS):
        rs = slice(r * rows, (r + 1) * rows)
        acc = jnp.dot(xn_ref[rs, :], w_ref[...], preferred_element_type=F32)
        for h in range(heads_per_tile):
            sl = slice(h * HEAD_DIM, (h + 1) * HEAD_DIM)
            y = acc[:, sl]
            inv = lax.rsqrt(jnp.mean(y * y, axis=-1, keepdims=True) + NORM_EPS)
            y = y * (inv * cols_ref[0:1, sl] + cols_ref[1:2, sl])
            if has_rope:
                rot = (y * cos_ref[rs, :] + pltpu.roll(y, 96, 1) * sa_ref[rs, :]
                       + pltpu.roll(y, 32, 1) * sb_ref[rs, :])
                y = y + cols_ref[2:3, sl] * (rot - y)
            o_ref[rs, sl] = (y * cols_ref[3:4, sl]).astype(o_ref.dtype)


def _normproj(x, g, w, head_ops, gains, rope_tables, *, col0, tm, tn):
    T, D = x.shape
    N = len(head_ops) * HEAD_DIM
    has_rope = rope_tables is not None
    zero, one = jnp.zeros((HEAD_DIM,), F32), jnp.ones((HEAD_DIM,), F32)
    cols = jnp.stack([
        jnp.concatenate([zero if gi is None else gains[gi] for gi, _, _ in head_ops]),
        jnp.concatenate([one if gi is None else zero for gi, _, _ in head_ops]),
        jnp.concatenate([one if rope else zero for _, rope, _ in head_ops]),
        jnp.concatenate([one * scale for _, _, scale in head_ops])])
    cols = jnp.concatenate([cols, jnp.zeros((4, N), F32)])
    jt0 = col0 // tn
    in_specs = [
        pl.BlockSpec((tm, D), lambda i, j: (i, 0)),
        pl.BlockSpec((1, D), lambda i, j: (0, 0)),
        pl.BlockSpec((D, tn), lambda i, j: (0, jt0 + j)),
        pl.BlockSpec((8, tn), lambda i, j: (0, j)),
    ]
    args = [x, g.reshape(1, D), w, cols]
    if has_rope:
        ns = rope_tables[0].shape[0] // tm
        for t in rope_tables:
            in_specs.append(pl.BlockSpec((tm, HEAD_DIM), lambda i, j: (i % ns, 0)))
            args.append(t)
    return pl.pallas_call(
        functools.partial(_normproj_kernel, heads_per_tile=tn // HEAD_DIM, has_rope=has_rope),
        grid=(T // tm, N // tn),
        in_specs=in_specs,
        out_specs=pl.BlockSpec((tm, tn), lambda i, j: (i, j)),
        out_shape=jax.ShapeDtypeStruct((T, N), BF16),
        scratch_shapes=[pltpu.VMEM((tm, D), BF16)],
        compiler_params=_params("parallel", "arbitrary"),
        name="normproj_rope" if has_rope else "normproj",
    )(*args)


def _rope_tables(S):
    pos = jnp.arange(S, dtype=jnp.int32)
    row = (pos // GRID_W).astype(F32)
    col = (pos % GRID_W).astype(F32)
    nf = HEAD_DIM // 4
    inv = ROPE_THETA ** (-jnp.arange(nf, dtype=F32) / nf)
    ang_row = row[:, None] * inv
    ang_col = col[:, None] * inv
    ang = jnp.concatenate([ang_row, ang_row, ang_col, ang_col], axis=-1)
    cos = jnp.cos(ang)
    sin = jnp.sin(ang)
    quarter = (jnp.arange(HEAD_DIM) // nf) % 2
    sa = jnp.where(quarter[None, :] == 0, -sin, 0.0)
    sb = jnp.where(quarter[None, :] == 1, sin, 0.0)
    return cos, sa, sb


def _gqa_kernel(q_ref, k_ref, v_ref, o_ref, m_ref, l_ref, acc_ref, *, group, nk):
    j = pl.program_id(3)

    @pl.when(j == 0)
    def _():
        m_ref[...] = jnp.full(m_ref.shape, NEG_INF, F32)
        l_ref[...] = jnp.zeros(l_ref.shape, F32)
        acc_ref[...] = jnp.zeros(acc_ref.shape, F32)

    k = k_ref[...]
    v = v_ref[...]
    for h in range(group):
        q = q_ref[:, h * HEAD_DIM:(h + 1) * HEAD_DIM]
        s = lax.dot_general(q, k, _NT, preferred_element_type=F32)
        m_prev = m_ref[h]
        m_new = jnp.maximum(m_prev, jnp.max(s, axis=1, keepdims=True))
        alpha = jnp.exp2(m_prev - m_new)
        p = jnp.exp2(s - m_new[:, :1])
        l_ref[h] = alpha * l_ref[h] + jnp.sum(p, axis=1, keepdims=True)
        acc_ref[h] = acc_ref[h] * alpha + jnp.dot(p.astype(BF16), v, preferred_element_type=F32)
        m_ref[h] = m_new

    @pl.when(j == nk - 1)
    def _():
        for h in range(group):
            o_ref[:, h * HEAD_DIM:(h + 1) * HEAD_DIM] = (acc_ref[h] / l_ref[h]).astype(o_ref.dtype)


def _gqa_bounded_kernel(q_ref, k_ref, v_ref, o_ref, l_ref, acc_ref, *, group, nk):
    j = pl.program_id(3)

    @pl.when(j == 0)
    def _():
        l_ref[...] = jnp.zeros(l_ref.shape, F32)
        acc_ref[...] = jnp.zeros(acc_ref.shape, F32)

    k = k_ref[...]
    v = v_ref[...]
    for h in range(group):
        q = q_ref[:, h * HEAD_DIM:(h + 1) * HEAD_DIM]
        p = jnp.exp2(lax.dot_general(q, k, _NT, preferred_element_type=F32))
        l_ref[h] += jnp.sum(p, axis=1, keepdims=True)
        acc_ref[h] += jnp.dot(p.astype(BF16), v, preferred_element_type=F32)

    @pl.when(j == nk - 1)
    def _():
        for h in range(group):
            o_ref[:, h * HEAD_DIM:(h + 1) * HEAD_DIM] = (acc_ref[h] / l_ref[h]).astype(o_ref.dtype)


def _gqa_attention(proj, B, S, *, q_col0, k_col0, v_col0, tq, tk, bounded):
    group = A_HEADS // A_KV_HEADS
    nq, nk = S // tq, S // tk
    gw = group * HEAD_DIM
    if bounded:
        body = functools.partial(_gqa_bounded_kernel, group=group, nk=nk)
        scratch = [pltpu.VMEM((group, tq, LANES), F32), pltpu.VMEM((group, tq, HEAD_DIM), F32)]
    else:
        body = functools.partial(_gqa_kernel, group=group, nk=nk)
        scratch = [pltpu.VMEM((group, tq, LANES), F32), pltpu.VMEM((group, tq, LANES), F32),
                   pltpu.VMEM((group, tq, HEAD_DIM), F32)]
    return pl.pallas_call(
        body,
        grid=(B, A_KV_HEADS, nq, nk),
        in_specs=[
            pl.BlockSpec((tq, gw), lambda b, g, i, j: (b * nq + i, q_col0 // gw + g)),
            pl.BlockSpec((tk, HEAD_DIM), lambda b, g, i, j: (b * nk + j, k_col0 // HEAD_DIM + g)),
            pl.BlockSpec((tk, HEAD_DIM), lambda b, g, i, j: (b * nk + j, v_col0 // HEAD_DIM + g)),
        ],
        out_specs=pl.BlockSpec((tq, gw), lambda b, g, i, j: (b * nq + i, g)),
        out_shape=jax.ShapeDtypeStruct((B * S, A_HEADS * HEAD_DIM), BF16),
        scratch_shapes=scratch,
        compiler_params=_params("parallel", "parallel", "parallel", "arbitrary"),
        name="gqa_bounded" if bounded else "gqa_attention",
    )(proj, proj, proj)


def _dilated_kernel(q_ref, *refs, nside, nq):
    nb = 2 * nside + 1
    k_refs, v_refs = refs[:nb], refs[nb:2 * nb]
    tbl_ref, o_ref = refs[2 * nb], refs[2 * nb + 1]
    i = pl.program_id(2)
    q = q_ref[...]
    scores = []
    m = None
    for d in range(nb):
        blk = i + (d - nside)
        in_range = (blk >= 0) & (blk < nq)
        s = lax.dot_general(q, k_refs[d][...], _NT, preferred_element_type=F32) + tbl_ref[0, d]
        s = jnp.where(in_range, s, NEG_INF)
        scores.append(s)
        md = jnp.max(s, axis=1, keepdims=True)
        m = md if m is None else jnp.maximum(m, md)
    l = None
    acc = None
    for d in range(nb):
        p = jnp.exp2(scores[d] - m)
        ld = jnp.sum(p, axis=1, keepdims=True)
        ad = jnp.dot(p.astype(BF16), v_refs[d][...], preferred_element_type=F32)
        l = ld if l is None else l + ld
        acc = ad if acc is None else acc + ad
    o_ref[...] = (acc / l).astype(o_ref.dtype)


def _dilated_bounded_kernel(q_ref, *refs, nside, nq):
    nb = 2 * nside + 1
    k_refs, v_refs = refs[:nb], refs[nb:2 * nb]
    tbl_ref, o_ref = refs[2 * nb], refs[2 * nb + 1]
    i = pl.program_id(2)
    q = q_ref[...]
    l = None
    acc = None
    for d in range(nb):
        blk = i + (d - nside)
        in_range = (blk >= 0) & (blk < nq)
        s = lax.dot_general(q, k_refs[d][...], _NT, preferred_element_type=F32) + tbl_ref[0, d]
        p = jnp.exp2(jnp.where(in_range, s, NEG_INF))
        ld = jnp.sum(p, axis=1, keepdims=True)
        ad = jnp.dot(p.astype(BF16), v_refs[d][...], preferred_element_type=F32)
        l = ld if l is None else l + ld
        acc = ad if acc is None else acc + ad
    o_ref[...] = (acc / l).astype(o_ref.dtype)


def _dilated_bias_table(slopes, tq, nside):
    nb = 2 * nside + 1
    a = jnp.arange(tq, dtype=jnp.int32)
    d = ((jnp.arange(nb, dtype=jnp.int32)[:, None, None] - nside) * tq
         + a[None, None, :] - a[None, :, None])
    ad = jnp.abs(d)
    count = jnp.zeros(d.shape, F32)
    for window, dil in B_PATTERNS:
        reach = (window // (2 * dil)) * dil
        count = count + ((ad % dil == 0) & (ad <= reach)).astype(F32)
    logc = jnp.where(count > 0, jnp.log2(jnp.maximum(count, 1.0)), NEG_INF)
    bias = -(slopes * LOG2E)[:, None, None, None] * ad.astype(F32)[None]
    return jnp.where(count[None] > 0, bias + logc[None], NEG_INF)


def _dilated_attention(proj, B, S, slopes, *, q_col0, k_col0, v_col0, tq, bounded):
    reach = max((w // (2 * dl)) * dl for w, dl in B_PATTERNS)
    nside = -(-reach // tq)
    nb = 2 * nside + 1
    nq = S // tq
    table = _dilated_bias_table(slopes, tq, nside)

    def kv_spec(col0, d):
        def imap(h, b, i):
            return (b * nq + jnp.clip(i + (d - nside), 0, nq - 1), col0 // HEAD_DIM + h)
        return pl.BlockSpec((tq, HEAD_DIM), imap)

    in_specs = [pl.BlockSpec((tq, HEAD_DIM), lambda h, b, i: (b * nq + i, q_col0 // HEAD_DIM + h))]
    in_specs += [kv_spec(k_col0, d) for d in range(nb)]
    in_specs += [kv_spec(v_col0, d) for d in range(nb)]
    in_specs += [pl.BlockSpec((1, nb, tq, tq), lambda h, b, i: (h, 0, 0, 0))]
    return pl.pallas_call(
        functools.partial(_dilated_bounded_kernel if bounded else _dilated_kernel,
                          nside=nside, nq=nq),
        grid=(B_HEADS, B, nq),
        in_specs=in_specs,
        out_specs=pl.BlockSpec((tq, HEAD_DIM), lambda h, b, i: (b * nq + i, h)),
        out_shape=jax.ShapeDtypeStruct((B * S, B_HEADS * HEAD_DIM), BF16),
        compiler_params=_params("parallel", "parallel", "parallel"),
        name="dilated_bounded" if bounded else "dilated_attention",
    )(*([proj] * (1 + 2 * nb)), table)


def _outproj_kernel(*refs, n_parts):
    a_refs = refs[:n_parts]
    w_refs = refs[n_parts:2 * n_parts]
    r_ref, o_ref = refs[2 * n_parts], refs[2 * n_parts + 1]
    acc = r_ref[...]
    for a_ref, w_ref in zip(a_refs, w_refs):
        acc = acc + jnp.dot(a_ref[...], w_ref[...], preferred_element_type=F32)
    o_ref[...] = acc


def _outproj_residual(parts, w, res, *, tm, tn):
    T, N = res.shape
    n_parts = len(parts)
    in_specs, w_args, off = [], [], 0
    for a in parts:
        kp = a.shape[1]
        in_specs.append(pl.BlockSpec((tm, kp), lambda i, j: (i, 0)))
        w_args.append((kp, off // kp))
        off += kp
    for kp, blk in w_args:
        in_specs.append(pl.BlockSpec((kp, tn), lambda i, j, blk=blk: (blk, j)))
    in_specs.append(pl.BlockSpec((tm, tn), lambda i, j: (i, j)))
    return pl.pallas_call(
        functools.partial(_outproj_kernel, n_parts=n_parts),
        grid=(T // tm, N // tn),
        in_specs=in_specs,
        out_specs=pl.BlockSpec((tm, tn), lambda i, j: (i, j)),
        out_shape=jax.ShapeDtypeStruct((T, N), F32),
        compiler_params=_params("parallel", "arbitrary"),
        name="outproj_residual",
    )(*parts, *([w] * n_parts), res)


def _silu(g):
    return g / (1.0 + jnp.exp(-g))


def _ffn_kernel(x_ref, g_ref, wg_ref, wu_ref, wd_ref, o_ref, xn_ref):
    @pl.when(pl.program_id(1) == 0)
    def _():
        x = x_ref[...]
        xn_ref[...] = (_rms(x) * g_ref[...]).astype(BF16)
        o_ref[...] = x

    xn = xn_ref[...]
    gate = jnp.dot(xn, wg_ref[...], preferred_element_type=F32)
    up = jnp.dot(xn, wu_ref[...], preferred_element_type=F32)
    hid = (_silu(gate) * up).astype(BF16)
    o_ref[...] += jnp.dot(hid, wd_ref[...], preferred_element_type=F32)


def _ffn_residual(x, g, w_gate, w_up, w_down, *, tm, tf):
    T, D = x.shape
    Fd = w_gate.shape[1]
    nf = Fd // tf
    return pl.pallas_call(
        _ffn_kernel,
        grid=(T // tm, nf),
        in_specs=[
            pl.BlockSpec((tm, D), lambda i, f: (i, 0)),
            pl.BlockSpec((1, D), lambda i, f: (0, 0)),
            pl.BlockSpec((D, tf), lambda i, f: (0, f)),
            pl.BlockSpec((D, tf), lambda i, f: (0, f)),
            pl.BlockSpec((tf, D), lambda i, f: (f, 0)),
        ],
        out_specs=pl.BlockSpec((tm, D), lambda i, f: (i, 0)),
        out_shape=jax.ShapeDtypeStruct((T, D), F32),
        scratch_shapes=[pltpu.VMEM((tm, D), BF16)],
        compiler_params=_params("parallel", "arbitrary"),
        name="ffn_residual",
    )(x, g.reshape(1, D), w_gate, w_up, w_down)


def _diff_kernel(slopes_ref, q_ref, k_ref, v_ref, dmat_ref, lq1_ref, lk1_ref, lq2_ref, lk2_ref,
                 subln_ref, o_ref, m_ref, l_ref, acc_ref, *, tq, tk, nk, lambda_init):
    h = pl.program_id(1)
    i = pl.program_id(2)
    j = pl.program_id(3)

    @pl.when(j == 0)
    def _():
        m_ref[...] = jnp.full(m_ref.shape, NEG_INF, F32)
        l_ref[...] = jnp.zeros(l_ref.shape, F32)
        acc_ref[...] = jnp.zeros(acc_ref.shape, F32)

    neg_slope = -slopes_ref[h] * LOG2E
    bias = _alibi_bias(dmat_ref[...], (j * tk - i * tq).astype(F32), neg_slope, tk)
    v = v_ref[...]
    for c in range(2):
        sl = slice(c * HEAD_DIM, (c + 1) * HEAD_DIM)
        s = lax.dot_general(q_ref[:, sl], k_ref[:, sl], _NT, preferred_element_type=F32) + bias
        m_prev = m_ref[c]
        m_new = jnp.maximum(m_prev, jnp.max(s, axis=1, keepdims=True))
        alpha = jnp.exp2(m_prev - m_new)
        p = jnp.exp2(s - m_new[:, :1])
        l_ref[c] = alpha * l_ref[c] + jnp.sum(p, axis=1, keepdims=True)
        acc_ref[c] = acc_ref[c] * alpha[:, :1] + jnp.dot(p.astype(BF16), v,
                                                         preferred_element_type=F32)
        m_ref[c] = m_new

    @pl.when(j == nk - 1)
    def _():
        _diff_finalize(acc_ref, l_ref, lq1_ref, lk1_ref, lq2_ref, lk2_ref, subln_ref, o_ref,
                       lambda_init)


def _alibi_bias(dbase, offset, neg_slope, tk):
    return jnp.concatenate([neg_slope * jnp.abs(dbase + (offset + float(cb * LANES)))
                            for cb in range(tk // LANES)], axis=1)


def _diff_finalize(acc_ref, l_ref, lq1_ref, lk1_ref, lq2_ref, lk2_ref, subln_ref, o_ref, lambda_init):
    lam = (jnp.exp(jnp.sum(lq1_ref[...] * lk1_ref[...], axis=-1, keepdims=True))
           - jnp.exp(jnp.sum(lq2_ref[...] * lk2_ref[...], axis=-1, keepdims=True))
           + lambda_init)
    o = acc_ref[0] / l_ref[0][:, :1] - lam * (acc_ref[1] / l_ref[1][:, :1])
    o = _rms(o) * subln_ref[...] * (1.0 - lambda_init)
    o_ref[...] = o.astype(o_ref.dtype)


def _diff_bounded_kernel(*refs, tq, tk, nk, lambda_init, n_cast):
    (slopes_ref, q_ref, k_ref, v_ref, dmat_ref, lq1_ref, lk1_ref, lq2_ref, lk2_ref,
     subln_ref) = refs[:10]
    cast_in, o_ref = refs[10:10 + n_cast], refs[10 + n_cast]
    cast_out = refs[11 + n_cast:11 + 2 * n_cast]
    l_ref, acc_ref = refs[-2:]
    for src, dst in zip(cast_in, cast_out):
        dst[...] = src[...].astype(dst.dtype)
    h = pl.program_id(1)
    i = pl.program_id(2)
    j = pl.program_id(3)

    @pl.when(j == 0)
    def _():
        l_ref[...] = jnp.zeros(l_ref.shape, F32)
        acc_ref[...] = jnp.zeros(acc_ref.shape, F32)

    neg_slope = -slopes_ref[h] * LOG2E
    q0 = i * tq
    k0 = j * tk
    v = v_ref[...]

    def scores(c):
        sl = slice(c * HEAD_DIM, (c + 1) * HEAD_DIM)
        return lax.dot_general(q_ref[:, sl], k_ref[:, sl], _NT, preferred_element_type=F32)

    overlaps = (k0 < q0 + tq) & (q0 < k0 + tk)

    @pl.when(overlaps)
    def _():
        for c in range(2):
            p = jnp.exp2(scores(c) + _alibi_bias(dmat_ref[...], (k0 - q0).astype(F32), neg_slope, tk))
            l_ref[c] += jnp.sum(p, axis=1, keepdims=True)
            acc_ref[c] += jnp.dot(p.astype(BF16), v, preferred_element_type=F32)

    @pl.when(jnp.logical_not(overlaps))
    def _():
        right = k0 >= q0 + tq
        gap = jnp.where(right, k0 - (q0 + tq), q0 - (k0 + tk)).astype(F32)
        key = lax.broadcasted_iota(jnp.int32, (1, tk), 1)
        qry = lax.broadcasted_iota(jnp.int32, (tq, 1), 0)
        key_bias = neg_slope * jnp.where(right, key, tk - key).astype(F32)
        qry_fac = jnp.exp2(neg_slope * (jnp.where(right, tq - qry, qry).astype(F32) + gap))
        for c in range(2):
            p = jnp.exp2(scores(c) + key_bias)
            l_ref[c] += qry_fac * jnp.sum(p, axis=1, keepdims=True)
            acc_ref[c] += qry_fac * jnp.dot(p.astype(BF16), v, preferred_element_type=F32)

    @pl.when(j == nk - 1)
    def _():
        _diff_finalize(acc_ref, l_ref, lq1_ref, lk1_ref, lq2_ref, lk2_ref, subln_ref, o_ref,
                       lambda_init)


BF16_SUBLANES = 16


def _cast_rows_per_step(n_rows, n_steps):
    for share in (1, 2, 4, 8):
        rows, rem = divmod(n_rows * share, n_steps)
        if rem == 0 and rows % BF16_SUBLANES == 0 and n_steps % share == 0:
            return rows, share
    return None


def _diff_attention(proj, B, S, slopes, lam_vecs, subln, lambda_init, *, tq, tk, bounded,
                    cast_2d=()):
    H = C_HEADS
    hw = 2 * HEAD_DIM
    nq, nk = S // tq, S // tk
    dmat = (jnp.arange(LANES, dtype=F32)[None, :] - jnp.arange(tq, dtype=F32)[:, None])
    stat = [pltpu.VMEM((2, tq, LANES), F32)]
    kw = dict(tq=tq, tk=tk, nk=nk, lambda_init=lambda_init)
    cast_specs = []
    if bounded:
        body, name = _diff_bounded_kernel, "diff_bounded"
        kw["n_cast"] = len(cast_2d)
        for a in cast_2d:
            rows, share = _cast_rows_per_step(a.shape[0], B * H * nq * nk)
            cast_specs.append(pl.BlockSpec(
                (rows, a.shape[1]),
                lambda b, h, i, j, s, share=share: ((((b * H + h) * nq + i) * nk + j) // share, 0)))
    else:
        assert not cast_2d
        body, name, stat = _diff_kernel, "diff_attention", stat * 2
    o_spec = pl.BlockSpec((tq, hw), lambda b, h, i, j, s: (b * nq + i, h))
    o_shape = jax.ShapeDtypeStruct((B * S, H * hw), BF16)
    vec_spec = pl.BlockSpec((1, HEAD_DIM), lambda b, h, i, j, s: (0, 0))
    grid_spec = pltpu.PrefetchScalarGridSpec(
        num_scalar_prefetch=1,
        grid=(B, H, nq, nk),
        in_specs=[
            pl.BlockSpec((tq, hw), lambda b, h, i, j, s: (b * nq + i, h)),
            pl.BlockSpec((tk, hw), lambda b, h, i, j, s: (b * nk + j, H + h)),
            pl.BlockSpec((tk, hw), lambda b, h, i, j, s: (b * nk + j, 2 * H + h)),
            pl.BlockSpec((tq, LANES), lambda b, h, i, j, s: (0, 0)),
            vec_spec, vec_spec, vec_spec, vec_spec,
            pl.BlockSpec((1, hw), lambda b, h, i, j, s: (0, 0)),
        ] + cast_specs,
        out_specs=[o_spec] + cast_specs if cast_specs else o_spec,
        scratch_shapes=stat + [pltpu.VMEM((2, tq, hw), F32)],
    )
    cast_shapes = [jax.ShapeDtypeStruct(a.shape, BF16) for a in cast_2d]
    return pl.pallas_call(
        functools.partial(body, **kw),
        grid_spec=grid_spec,
        out_shape=[o_shape] + cast_shapes if cast_specs else o_shape,
        compiler_params=_params("arbitrary", "arbitrary", "arbitrary", "arbitrary"),
        name=name,
    )(slopes, proj, proj, proj, dmat, *[v.reshape(1, HEAD_DIM) for v in lam_vecs],
      subln.reshape(1, hw), *cast_2d)


def _router_kernel(x_ref, g_ref, wr_ref, meta_ref, cnt_ref, base_ref, *, tm, n_exp):
    i = pl.program_id(0)

    @pl.when(i == 0)
    def _():
        base_ref[...] = jnp.zeros(base_ref.shape, F32)

    hn = _rms(x_ref[...]) * g_ref[...]
    logits = jnp.dot(hn, wr_ref[...], preferred_element_type=F32,
                     precision=lax.Precision.HIGHEST)
    lane = lax.broadcasted_iota(jnp.int32, (tm, LANES), 1)
    logits = jnp.where(lane < n_exp, logits, NEG_INF)
    t1 = jnp.max(logits, axis=1, keepdims=True)
    i1 = jnp.min(jnp.where(logits == t1, lane, LANES), axis=1, keepdims=True)
    rest = jnp.where(lane == i1, NEG_INF, logits)
    t2 = jnp.max(rest, axis=1, keepdims=True)
    i2 = jnp.min(jnp.where(rest == t2, lane, LANES), axis=1, keepdims=True)
    ex = jnp.exp(t2 - t1)
    g1 = 1.0 / (1.0 + ex)
    g2 = ex * g1
    oh1 = lane == i1
    oh2 = lane == i2
    member = jnp.where(oh1 | oh2, 1.0, 0.0)
    row = lax.broadcasted_iota(jnp.int32, (tm, tm), 0)
    col = lax.broadcasted_iota(jnp.int32, (tm, tm), 1)
    strict_lower = jnp.where(col < row, 1.0, 0.0).astype(BF16)
    before = jnp.dot(strict_lower, member.astype(BF16), preferred_element_type=F32) + base_ref[...]
    r1 = jnp.sum(jnp.where(oh1, before, 0.0), axis=1, keepdims=True)
    r2 = jnp.sum(jnp.where(oh2, before, 0.0), axis=1, keepdims=True)
    base_ref[...] += jnp.sum(member, axis=0, keepdims=True)
    meta = jnp.where(lane == 0, i1.astype(F32), 0.0)
    meta = jnp.where(lane == 1, i2.astype(F32), meta)
    meta = jnp.where(lane == 2, g1, meta)
    meta = jnp.where(lane == 3, g2, meta)
    meta = jnp.where(lane == 4, r1, meta)
    meta = jnp.where(lane == 5, r2, meta)
    meta_ref[...] = meta
    cnt_ref[...] = base_ref[...]


def _router(x, g, w_router, *, tm):
    T, D = x.shape
    n_exp = w_router.shape[1]
    wr = jnp.zeros((D, LANES), F32).at[:, :n_exp].set(w_router)
    return pl.pallas_call(
        functools.partial(_router_kernel, tm=tm, n_exp=n_exp),
        grid=(T // tm,),
        in_specs=[
            pl.BlockSpec((tm, D), lambda i: (i, 0)),
            pl.BlockSpec((1, D), lambda i: (0, 0)),
            pl.BlockSpec((D, LANES), lambda i: (0, 0)),
        ],
        out_specs=[pl.BlockSpec((tm, LANES), lambda i: (i, 0)),
                   pl.BlockSpec((1, LANES), lambda i: (0, 0))],
        out_shape=[jax.ShapeDtypeStruct((T, LANES), F32),
                   jax.ShapeDtypeStruct((1, LANES), F32)],
        scratch_shapes=[pltpu.VMEM((1, LANES), F32)],
        compiler_params=_params("arbitrary"),
        name="moe_router",
    )(x, g.reshape(1, D), wr)


def _row_copy(src_ref, src_row, dst_ref, dst_row, sem):
    return pltpu.make_async_copy(src_ref.at[pl.ds(src_row, 1)], dst_ref.at[pl.ds(dst_row, 1)], sem)


def _dispatch_kernel(pend_ref, dest_hbm, x_ref, xs_hbm, idx_ref, zero_ref, idx_sem, row_sem, zero_sem,
                     *, tm, bm, n_rows):
    i = pl.program_id(0)

    @pl.when(i == 0)
    def _():
        zero_ref[...] = jnp.zeros(zero_ref.shape, F32)
        fills = [pltpu.make_async_copy(zero_ref, xs_hbm.at[pl.ds(n_rows - (e + 1) * bm, bm)], zero_sem)
                 for e in range(N_EXPERTS)]
        for e in range(N_EXPERTS):
            start = pl.multiple_of(jnp.maximum(pend_ref[e] - bm, 0), 8)
            fills.append(pltpu.make_async_copy(zero_ref, xs_hbm.at[pl.ds(start, bm)], zero_sem))
        for f in fills:
            f.start()
            f.wait()

    n_idx = TOP_K * tm
    idx_copy = pltpu.make_async_copy(dest_hbm.at[pl.ds(i * n_idx, n_idx)], idx_ref, idx_sem)
    idx_copy.start()
    idx_copy.wait()

    def issue(t, carry):
        for k in range(TOP_K):
            _row_copy(x_ref, t, xs_hbm, idx_ref[TOP_K * t + k], row_sem).start()
        return carry

    lax.fori_loop(0, tm, issue, 0)

    for k in range(TOP_K):
        pltpu.make_async_copy(x_ref, xs_hbm.at[pl.ds(0, tm)], row_sem).wait()


def _dispatch(dest, x, pend, n_rows, *, tm, bm):
    T, D = x.shape
    grid_spec = pltpu.PrefetchScalarGridSpec(
        num_scalar_prefetch=1,
        grid=(T // tm,),
        in_specs=[pl.BlockSpec(memory_space=pl.ANY),
                  pl.BlockSpec((tm, D), lambda i, pend: (i, 0))],
        out_specs=pl.BlockSpec(memory_space=pl.ANY),
        scratch_shapes=[pltpu.SMEM((TOP_K * tm,), jnp.int32), pltpu.VMEM((bm, D), F32),
                        pltpu.SemaphoreType.DMA(()), pltpu.SemaphoreType.DMA(()),
                        pltpu.SemaphoreType.DMA(())],
    )
    return pl.pallas_call(
        functools.partial(_dispatch_kernel, tm=tm, bm=bm, n_rows=n_rows),
        grid_spec=grid_spec,
        out_shape=jax.ShapeDtypeStruct((n_rows, D), F32),
        compiler_params=pltpu.CompilerParams(dimension_semantics=("arbitrary",),
                                             vmem_limit_bytes=VMEM_LIMIT_BYTES,
                                             has_side_effects=True),
        name="moe_dispatch",
    )(pend.astype(jnp.int32), dest, x)


def _expert_kernel(be_ref, bn_ref, x_ref, g_ref, wg_ref, wu_ref, wd_ref, y_ref, xn_ref, *, bm):
    del be_ref
    n_valid = bn_ref[pl.program_id(0)]

    @pl.when(pl.program_id(1) == 0)
    def _():
        rows = lax.broadcasted_iota(jnp.int32, (bm, 1), 0)
        x = jnp.where(rows < n_valid, x_ref[...], 0.0)
        xn_ref[...] = (_rms(x) * g_ref[...]).astype(BF16)
        y_ref[...] = jnp.zeros(y_ref.shape, F32)

    def ffn_rows(rs):
        xn = xn_ref[rs, :]
        gate = jnp.dot(xn, wg_ref[...], preferred_element_type=F32)
        up = jnp.dot(xn, wu_ref[...], preferred_element_type=F32)
        hid = (_silu(gate) * up).astype(BF16)
        y_ref[rs, :] += jnp.dot(hid, wd_ref[...], preferred_element_type=F32)

    half = bm // 2

    @pl.when(n_valid > half)
    def _():
        ffn_rows(slice(0, bm))

    @pl.when((n_valid > 0) & (n_valid <= half))
    def _():
        ffn_rows(slice(0, half))


def _expert_ffn(xs, g, e_gate, e_up, e_down, blk_expert, blk_valid, *, bm, tf):
    P, D = xs.shape
    Fd = e_gate.shape[2]
    nf = Fd // tf
    nblk = P // bm

    def f_idx(b, f, bn):
        return jnp.where(bn[b] > 0, f, nf - 1)

    grid_spec = pltpu.PrefetchScalarGridSpec(
        num_scalar_prefetch=2,
        grid=(nblk, nf),
        in_specs=[
            pl.BlockSpec((bm, D), lambda b, f, be, bn: (b, 0)),
            pl.BlockSpec((1, D), lambda b, f, be, bn: (0, 0)),
            pl.BlockSpec((None, D, tf), lambda b, f, be, bn: (be[b], 0, f_idx(b, f, bn))),
            pl.BlockSpec((None, D, tf), lambda b, f, be, bn: (be[b], 0, f_idx(b, f, bn))),
            pl.BlockSpec((None, tf, D), lambda b, f, be, bn: (be[b], f_idx(b, f, bn), 0)),
        ],
        out_specs=pl.BlockSpec((bm, D), lambda b, f, be, bn: (b, 0)),
        scratch_shapes=[pltpu.VMEM((bm, D), BF16)],
    )
    return pl.pallas_call(
        functools.partial(_expert_kernel, bm=bm),
        grid_spec=grid_spec,
        out_shape=jax.ShapeDtypeStruct((P, D), F32),
        compiler_params=_params("arbitrary", "arbitrary"),
        name="moe_experts",
    )(blk_expert, blk_valid, xs, g.reshape(1, D), e_gate, e_up, e_down)


def _combine_kernel(dest_hbm, ys_hbm, x_ref, meta_ref, o_ref, idx_ref, ybuf_ref, idx_sem, row_sem,
                    *, tm):
    i = pl.program_id(0)
    n_idx = TOP_K * tm
    idx_copy = pltpu.make_async_copy(dest_hbm.at[pl.ds(i * n_idx, n_idx)], idx_ref, idx_sem)
    idx_copy.start()
    idx_copy.wait()

    def issue(t, carry):
        for k in range(TOP_K):
            _row_copy(ys_hbm, idx_ref[TOP_K * t + k], ybuf_ref.at[k], t, row_sem).start()
        return carry

    lax.fori_loop(0, tm, issue, 0)

    for k in range(TOP_K):
        pltpu.make_async_copy(ys_hbm.at[pl.ds(0, tm)], ybuf_ref.at[k], row_sem).wait()

    meta = meta_ref[...]
    out = x_ref[...]
    for k in range(TOP_K):
        out = out + meta[:, 2 + k:3 + k] * ybuf_ref[k]
    o_ref[...] = out


def _combine(dest, ys, x, meta, *, tm):
    T, D = x.shape
    return pl.pallas_call(
        functools.partial(_combine_kernel, tm=tm),
        grid=(T // tm,),
        in_specs=[pl.BlockSpec(memory_space=pl.ANY), pl.BlockSpec(memory_space=pl.ANY),
                  pl.BlockSpec((tm, D), lambda i: (i, 0)),
                  pl.BlockSpec((tm, LANES), lambda i: (i, 0))],
        out_specs=pl.BlockSpec((tm, D), lambda i: (i, 0)),
        out_shape=jax.ShapeDtypeStruct((T, D), F32),
        scratch_shapes=[pltpu.SMEM((TOP_K * tm,), jnp.int32),
                        pltpu.VMEM((TOP_K, tm, D), F32),
                        pltpu.SemaphoreType.DMA(()), pltpu.SemaphoreType.DMA(())],
        compiler_params=_params("arbitrary"),
        name="moe_combine",
    )(dest, ys, x, meta)


def _moe_residual(x, g, w_router, e_gate, e_up, e_down, *, tm_route, tm_move, bm, tf):
    T, D = x.shape
    meta, counts = _router(x, g, w_router, tm=tm_route)
    counts = counts[0, :N_EXPERTS].astype(jnp.int32)
    padded = (counts + bm - 1) // bm * bm
    pend = jnp.cumsum(padded)
    pstart = pend - padded
    expert = meta[:, 0:TOP_K].astype(jnp.int32)
    rank = meta[:, 4:4 + TOP_K].astype(jnp.int32)
    dest = (pstart[expert] + rank).reshape(T * TOP_K)
    n_rows = T * TOP_K + N_EXPERTS * bm
    nblk = n_rows // bm
    blk_row0 = jnp.arange(nblk, dtype=jnp.int32) * bm
    blk_expert = jnp.minimum(jnp.searchsorted(pend, blk_row0, side='right'),
                             N_EXPERTS - 1).astype(jnp.int32)
    blk_valid = jnp.clip(counts[blk_expert] - (blk_row0 - pstart[blk_expert]), 0, bm)
    blk_valid = jnp.where(blk_row0 < pend[-1], blk_valid, 0).astype(jnp.int32)
    last_used = jnp.maximum(pend[-1] // bm - 1, 0)
    blk_expert = jnp.where(blk_row0 < pend[-1], blk_expert, blk_expert[last_used])

    xs = _dispatch(dest, x, pend, n_rows, tm=tm_move, bm=bm)
    ys = _expert_ffn(xs, g, e_gate, e_up, e_down, blk_expert, blk_valid, bm=bm, tf=tf)
    return _combine(dest, ys, x, meta, tm=tm_move)


def _alibi_slopes(n_heads):
    return 2.0 ** (-8.0 * jnp.arange(1, n_heads + 1, dtype=F32) / n_heads)


def _tile(n, want):
    t = min(n, want)
    while n % t:
        t //= 2
    return t


Q_SCALE = HEAD_DIM ** -0.5 * LOG2E
PROJ_TN = 512
BOUNDED_SCORE_LIMIT = 60.0


class _Tiles(NamedTuple):
    rows: int
    out_cols: int
    attn_q: int
    attn_k: int
    online_q: int
    online_k: int
    band_q: int
    ffn_rows: int
    ffn_hidden: int
    moe_rows: int
    moe_hidden: int


def _tiles(S, T, d_ff_expert=1024):
    return _Tiles(rows=_tile(S, 1024), out_cols=1024,
                  attn_q=_tile(S, 1024), attn_k=_tile(S, 4096),
                  online_q=_tile(S, 512), online_k=_tile(S, 1024), band_q=_tile(S, 512),
                  ffn_rows=_tile(T, 512), ffn_hidden=512,
                  moe_rows=_tile(T, 512), moe_hidden=_tile(d_ff_expert, 1024))


def _score_bound(q_gain, k_gain):
    return 1.01 * HEAD_DIM * Q_SCALE * jnp.max(jnp.abs(q_gain)) * jnp.max(jnp.abs(k_gain))


def _layer0(x, norm_mix, w_in, qn_a, kn_a, qn_b, kn_b, w_out, norm_ffn, w_gate, w_up, w_down):
    B, S, D = x.shape
    T = B * S
    xf = x.reshape(T, D)
    t = _tiles(S, T)
    a_q, a_kv, b_dim = A_HEADS * HEAD_DIM, A_KV_HEADS * HEAD_DIM, B_HEADS * HEAD_DIM
    w_in = w_in.astype(BF16)
    ops_a = ([(0, True, Q_SCALE)] * A_HEADS + [(1, True, 1.0)] * A_KV_HEADS
             + [(None, False, 1.0)] * A_KV_HEADS)
    ops_b = [(0, False, Q_SCALE)] * B_HEADS + [(1, False, 1.0)] * B_HEADS + [(None, False, 1.0)] * B_HEADS
    proj_a = _normproj(xf, norm_mix, w_in, ops_a, [qn_a, kn_a], _rope_tables(S), col0=0,
                       tm=t.rows, tn=PROJ_TN)
    proj_b = _normproj(xf, norm_mix, w_in, ops_b, [qn_b, kn_b], None, col0=a_q + 2 * a_kv,
                       tm=t.rows, tn=PROJ_TN)
    gqa = functools.partial(_gqa_attention, B=B, S=S, q_col0=0, k_col0=a_q, v_col0=a_q + a_kv)
    o_a = lax.cond(_score_bound(qn_a, kn_a) <= BOUNDED_SCORE_LIMIT,
                   lambda p: gqa(p, tq=t.attn_q, tk=t.attn_k, bounded=True),
                   lambda p: gqa(p, tq=t.online_q, tk=t.online_k, bounded=False), proj_a)
    dil = functools.partial(_dilated_attention, B=B, S=S, slopes=_alibi_slopes(B_HEADS), q_col0=0,
                            k_col0=b_dim, v_col0=2 * b_dim, tq=t.band_q)
    o_b = lax.cond(_score_bound(qn_b, kn_b) <= BOUNDED_SCORE_LIMIT,
                   lambda p: dil(p, bounded=True), lambda p: dil(p, bounded=False), proj_b)
    x1 = _outproj_residual([o_a, o_b], w_out.astype(BF16), xf, tm=t.rows, tn=t.out_cols)
    x2 = _ffn_residual(x1, norm_ffn, w_gate.astype(BF16), w_up.astype(BF16), w_down.astype(BF16),
                       tm=t.ffn_rows, tf=t.ffn_hidden)
    return x2.reshape(B, S, D)


def _layer1(x, norm_mix, w_in, qn_c, kn_c, lam_q1, lam_k1, lam_q2, lam_k2, subln, w_out, norm_ffn,
            w_router, e_gate, e_up, e_down):
    B, S, D = x.shape
    T = B * S
    xf = x.reshape(T, D)
    t = _tiles(S, T, e_gate.shape[2])
    head_ops = ([(0, False, Q_SCALE)] * (2 * C_HEADS) + [(1, False, 1.0)] * (2 * C_HEADS)
                + [(None, False, 1.0)] * (2 * C_HEADS))
    proj = _normproj(xf, norm_mix, w_in.astype(BF16), head_ops, [qn_c, kn_c], None, col0=0,
                     tm=t.rows, tn=2 * PROJ_TN)
    lambda_init = 0.8 - 0.6 * math.exp(-0.3 * 1)
    diff = functools.partial(_diff_attention, B=B, S=S, slopes=_alibi_slopes(C_HEADS),
                             lam_vecs=(lam_q1, lam_k1, lam_q2, lam_k2), subln=subln,
                             lambda_init=lambda_init)
    tq, tk = t.attn_q, t.attn_k
    n_steps = B * C_HEADS * (S // tq) * (S // tk)
    rows_of = lambda w: math.prod(w.shape[:-1])
    stream_casts = all(_cast_rows_per_step(rows_of(w), n_steps) for w in (e_gate, e_up, e_down))

    def bounded_branch(p, *ws):
        if not stream_casts:
            return (diff(p, tq=tq, tk=tk, bounded=True), *[w.astype(BF16) for w in ws])
        o, *cast = diff(p, tq=tq, tk=tk, bounded=True,
                        cast_2d=[w.reshape(-1, w.shape[-1]) for w in ws])
        return (o, *[c.reshape(w.shape) for c, w in zip(cast, ws)])

    def online_branch(p, *ws):
        return (diff(p, tq=t.online_q, tk=t.online_k, bounded=False),
                *[w.astype(BF16) for w in ws])

    o_c, eg, eu, ed = lax.cond(_score_bound(qn_c, kn_c) <= BOUNDED_SCORE_LIMIT,
                               bounded_branch, online_branch, proj, e_gate, e_up, e_down)
    x3 = _outproj_residual([o_c], w_out.astype(BF16), xf, tm=t.rows, tn=t.out_cols)
    out = _moe_residual(x3, norm_ffn, w_router, eg, eu, ed, tm_route=t.moe_rows,
                        tm_move=t.moe_rows, bm=t.moe_rows, tf=t.moe_hidden)
    return out.reshape(B, S, D)


def kernel(x, l0_norm_mix,l0_w_in, l0_qnorm_a, l0_knorm_a, l0_qnorm_b, l0_knorm_b, l0_w_out, l0_norm_ffn, l0_w_gate, l0_w_up, l0_w_down, l1_norm_mix, l1_w_in, l1_qnorm_c, l1_knorm_c, l1_lambda_q1, l1_lambda_k1, l1_lambda_q2, l1_lambda_k2, l1_subln, l1_w_out, l1_norm_ffn, l1_w_router, l1_e_gate, l1_e_up, l1_e_down):
    x = _layer0(x, l0_norm_mix, l0_w_in, l0_qnorm_a, l0_knorm_a, l0_qnorm_b, l0_knorm_b, l0_w_out,
                l0_norm_ffn, l0_w_gate, l0_w_up, l0_w_down)
    return _layer1(x, l1_norm_mix, l1_w_in, l1_qnorm_c, l1_knorm_c, l1_lambda_q1, l1_lambda_k1,
                   l1_lambda_q2, l1_lambda_k2, l1_subln, l1_w_out, l1_norm_ffn, l1_w_router,
                   l1_e_gate, l1_e_up, l1_e_down)
```

```python
import functools
import math
from typing import NamedTuple

import jax
import jax.numpy as jnp
from jax import lax
from jax.experimental import pallas as pl
from jax.experimental.pallas import tpu as pltpu

F32 = jnp.float32
BF16 = jnp.bfloat16

HEAD_DIM = 128
LANES = 128
A_HEADS = 8
A_KV_HEADS = 2
B_HEADS = 8
B_PATTERNS = ((128, 1), (512, 4), (2048, 16))
C_HEADS = 8
N_EXPERTS = 8
TOP_K = 2
GRID_W = 64
ROPE_THETA = 10000.0
NORM_EPS = 1e-6
NEG_INF = -1e30
LOG2E = 1.4426950408889634
VMEM_LIMIT_BYTES = 56 * 1024 * 1024

_NT = (((1,), (1,)), ((), ()))


def _params(*sem):
    return pltpu.CompilerParams(dimension_semantics=sem, vmem_limit_bytes=VMEM_LIMIT_BYTES)


def _rms(x, eps=NORM_EPS):
    return x * lax.rsqrt(jnp.mean(x * x, axis=-1, keepdims=True) + eps)


PROJ_ROW_CHUNKS = 2


def _normproj_kernel(*refs, heads_per_tile, has_rope):
    if has_rope:
        x_ref, g_ref, w_ref, cols_ref, cos_ref, sa_ref, sb_ref, o_ref, xn_ref = refs
    else:
        x_ref, g_ref, w_ref, cols_ref, o_ref, xn_ref = refs

    @pl.when(pl.program_id(1) == 0)
    def _():
        xn_ref[...] = (_rms(x_ref[...]) * g_ref[...]).astype(BF16)

    rows = x_ref.shape[0] // PROJ_ROW_CHUNKS
    for r in range(PROJ_ROW_CHUNKS):
        rs = slice(r * rows, (r + 1) * rows)
        acc = jnp.dot(xn_ref[rs, :], w_ref[...], preferred_element_type=F32)
        for h in range(heads_per_tile):
            sl = slice(h * HEAD_DIM, (h + 1) * HEAD_DIM)
            y = acc[:, sl]
            inv = lax.rsqrt(jnp.mean(y * y, axis=-1, keepdims=True) + NORM_EPS)
            y = y * (inv * cols_ref[0:1, sl] + cols_ref[1:2, sl])
            if has_rope:
                rot = (y * cos_ref[rs, :] + pltpu.roll(y, 96, 1) * sa_ref[rs, :]
                       + pltpu.roll(y, 32, 1) * sb_ref[rs, :])
                y = y + cols_ref[2:3, sl] * (rot - y)
            o_ref[rs, sl] = (y * cols_ref[3:4, sl]).astype(o_ref.dtype)


def _normproj(x, g, w, head_ops, gains, rope_tables, *, col0, tm, tn):
    T, D = x.shape
    N = len(head_ops) * HEAD_DIM
    has_rope = rope_tables is not None
    zero, one = jnp.zeros((HEAD_DIM,), F32), jnp.ones((HEAD_DIM,), F32)
    cols = jnp.stack([
        jnp.concatenate([zero if gi is None else gains[gi] for gi, _, _ in head_ops]),
        jnp.concatenate([one if gi is None else zero for gi, _, _ in head_ops]),
        jnp.concatenate([one if rope else zero for _, rope, _ in head_ops]),
        jnp.concatenate([one * scale for _, _, scale in head_ops])])
    cols = jnp.concatenate([cols, jnp.zeros((4, N), F32)])
    jt0 = col0 // tn
    in_specs = [
        pl.BlockSpec((tm, D), lambda i, j: (i, 0)),
        pl.BlockSpec((1, D), lambda i, j: (0, 0)),
        pl.BlockSpec((D, tn), lambda i, j: (0, jt0 + j)),
        pl.BlockSpec((8, tn), lambda i, j: (0, j)),
    ]
    args = [x, g.reshape(1, D), w, cols]
    if has_rope:
        ns = rope_tables[0].shape[0] // tm
        for t in rope_tables:
            in_specs.append(pl.BlockSpec((tm, HEAD_DIM), lambda i, j: (i % ns, 0)))
            args.append(t)
    return pl.pallas_call(
        functools.partial(_normproj_kernel, heads_per_tile=tn // HEAD_DIM, has_rope=has_rope),
        grid=(T // tm, N // tn),
        in_specs=in_specs,
        out_specs=pl.BlockSpec((tm, tn), lambda i, j: (i, j)),
        out_shape=jax.ShapeDtypeStruct((T, N), BF16),
        scratch_shapes=[pltpu.VMEM((tm, D), BF16)],
        compiler_params=_params("parallel", "arbitrary"),
        name="normproj_rope" if has_rope else "normproj",
    )(*args)


def _rope_tables(S):
    pos = jnp.arange(S, dtype=jnp.int32)
    row = (pos // GRID_W).astype(F32)
    col = (pos % GRID_W).astype(F32)
    nf = HEAD_DIM // 4
    inv = ROPE_THETA ** (-jnp.arange(nf, dtype=F32) / nf)
    ang_row = row[:, None] * inv
    ang_col = col[:, None] * inv
    ang = jnp.concatenate([ang_row, ang_row, ang_col, ang_col], axis=-1)
    cos = jnp.cos(ang)
    sin = jnp.sin(ang)
    quarter = (jnp.arange(HEAD_DIM) // nf) % 2
    sa = jnp.where(quarter[None, :] == 0, -sin, 0.0)
    sb = jnp.where(quarter[None, :] == 1, sin, 0.0)
    return cos, sa, sb


def _gqa_kernel(q_ref, k_ref, v_ref, o_ref, m_ref, l_ref, acc_ref, *, group, nk):
    j = pl.program_id(3)

    @pl.when(j == 0)
    def _():
        m_ref[...] = jnp.full(m_ref.shape, NEG_INF, F32)
        l_ref[...] = jnp.zeros(l_ref.shape, F32)
        acc_ref[...] = jnp.zeros(acc_ref.shape, F32)

    k = k_ref[...]
    v = v_ref[...]
    for h in range(group):
        q = q_ref[:, h * HEAD_DIM:(h + 1) * HEAD_DIM]
        s = lax.dot_general(q, k, _NT, preferred_element_type=F32)
        m_prev = m_ref[h]
        m_new = jnp.maximum(m_prev, jnp.max(s, axis=1, keepdims=True))
        alpha = jnp.exp2(m_prev - m_new)
        p = jnp.exp2(s - m_new[:, :1])
        l_ref[h] = alpha * l_ref[h] + jnp.sum(p, axis=1, keepdims=True)
        acc_ref[h] = acc_ref[h] * alpha + jnp.dot(p.astype(BF16), v, preferred_element_type=F32)
        m_ref[h] = m_new

    @pl.when(j == nk - 1)
    def _():
        for h in range(group):
            o_ref[:, h * HEAD_DIM:(h + 1) * HEAD_DIM] = (acc_ref[h] / l_ref[h]).astype(o_ref.dtype)


def _gqa_bounded_kernel(q_ref, k_ref, v_ref, o_ref, l_ref, acc_ref, *, group, nk):
    j = pl.program_id(3)

    @pl.when(j == 0)
    def _():
        l_ref[...] = jnp.zeros(l_ref.shape, F32)
        acc_ref[...] = jnp.zeros(acc_ref.shape, F32)

    k = k_ref[...]
    v = v_ref[...]
    for h in range(group):
        q = q_ref[:, h * HEAD_DIM:(h + 1) * HEAD_DIM]
        p = jnp.exp2(lax.dot_general(q, k, _NT, preferred_element_type=F32))
        l_ref[h] += jnp.sum(p, axis=1, keepdims=True)
        acc_ref[h] += jnp.dot(p.astype(BF16), v, preferred_element_type=F32)

    @pl.when(j == nk - 1)
    def _():
        for h in range(group):
            o_ref[:, h * HEAD_DIM:(h + 1) * HEAD_DIM] = (acc_ref[h] / l_ref[h]).astype(o_ref.dtype)


def _gqa_attention(proj, B, S, *, q_col0, k_col0, v_col0, tq, tk, bounded):
    group = A_HEADS // A_KV_HEADS
    nq, nk = S // tq, S // tk
    gw = group * HEAD_DIM
    if bounded:
        body = functools.partial(_gqa_bounded_kernel, group=group, nk=nk)
        scratch = [pltpu.VMEM((group, tq, LANES), F32), pltpu.VMEM((group, tq, HEAD_DIM), F32)]
    else:
        body = functools.partial(_gqa_kernel, group=group, nk=nk)
        scratch = [pltpu.VMEM((group, tq, LANES), F32), pltpu.VMEM((group, tq, LANES), F32),
                   pltpu.VMEM((group, tq, HEAD_DIM), F32)]
    return pl.pallas_call(
        body,
        grid=(B, A_KV_HEADS, nq, nk),
        in_specs=[
            pl.BlockSpec((tq, gw), lambda b, g, i, j: (b * nq + i, q_col0 // gw + g)),
            pl.BlockSpec((tk, HEAD_DIM), lambda b, g, i, j: (b * nk + j, k_col0 // HEAD_DIM + g)),
            pl.BlockSpec((tk, HEAD_DIM), lambda b, g, i, j: (b * nk + j, v_col0 // HEAD_DIM + g)),
        ],
        out_specs=pl.BlockSpec((tq, gw), lambda b, g, i, j: (b * nq + i, g)),
        out_shape=jax.ShapeDtypeStruct((B * S, A_HEADS * HEAD_DIM), BF16),
        scratch_shapes=scratch,
        compiler_params=_params("parallel", "parallel", "parallel", "arbitrary"),
        name="gqa_bounded" if bounded else "gqa_attention",
    )(proj, proj, proj)


def _dilated_kernel(q_ref, *refs, nside, nq):
    nb = 2 * nside + 1
    k_refs, v_refs = refs[:nb], refs[nb:2 * nb]
    tbl_ref, o_ref = refs[2 * nb], refs[2 * nb + 1]
    i = pl.program_id(2)
    q = q_ref[...]
    scores = []
    m = None
    for d in range(nb):
        blk = i + (d - nside)
        in_range = (blk >= 0) & (blk < nq)
        s = lax.dot_general(q, k_refs[d][...], _NT, preferred_element_type=F32) + tbl_ref[0, d]
        s = jnp.where(in_range, s, NEG_INF)
        scores.append(s)
        md = jnp.max(s, axis=1, keepdims=True)
        m = md if m is None else jnp.maximum(m, md)
    l = None
    acc = None
    for d in range(nb):
        p = jnp.exp2(scores[d] - m)
        ld = jnp.sum(p, axis=1, keepdims=True)
        ad = jnp.dot(p.astype(BF16), v_refs[d][...], preferred_element_type=F32)
        l = ld if l is None else l + ld
        acc = ad if acc is None else acc + ad
    o_ref[...] = (acc / l).astype(o_ref.dtype)


def _dilated_bounded_kernel(q_ref, *refs, nside, nq):
    nb = 2 * nside + 1
    k_refs, v_refs = refs[:nb], refs[nb:2 * nb]
    tbl_ref, o_ref = refs[2 * nb], refs[2 * nb + 1]
    i = pl.program_id(2)
    q = q_ref[...]
    l = None
    acc = None
    for d in range(nb):
        blk = i + (d - nside)
        in_range = (blk >= 0) & (blk < nq)
        s = lax.dot_general(q, k_refs[d][...], _NT, preferred_element_type=F32) + tbl_ref[0, d]
        p = jnp.exp2(jnp.where(in_range, s, NEG_INF))
        ld = jnp.sum(p, axis=1, keepdims=True)
        ad = jnp.dot(p.astype(BF16), v_refs[d][...], preferred_element_type=F32)
        l = ld if l is None else l + ld
        acc = ad if acc is None else acc + ad
    o_ref[...] = (acc / l).astype(o_ref.dtype)


def _dilated_bias_table(slopes, tq, nside):
    nb = 2 * nside + 1
    a = jnp.arange(tq, dtype=jnp.int32)
    d = ((jnp.arange(nb, dtype=jnp.int32)[:, None, None] - nside) * tq
         + a[None, None, :] - a[None, :, None])
    ad = jnp.abs(d)
    count = jnp.zeros(d.shape, F32)
    for window, dil in B_PATTERNS:
        reach = (window // (2 * dil)) * dil
        count = count + ((ad % dil == 0) & (ad <= reach)).astype(F32)
    logc = jnp.where(count > 0, jnp.log2(jnp.maximum(count, 1.0)), NEG_INF)
    bias = -(slopes * LOG2E)[:, None, None, None] * ad.astype(F32)[None]
    return jnp.where(count[None] > 0, bias + logc[None], NEG_INF)


def _dilated_attention(proj, B, S, slopes, *, q_col0, k_col0, v_col0, tq, bounded):
    reach = max((w // (2 * dl)) * dl for w, dl in B_PATTERNS)
    nside = -(-reach // tq)
    nb = 2 * nside + 1
    nq = S // tq
    table = _dilated_bias_table(slopes, tq, nside)

    def kv_spec(col0, d):
        def imap(h, b, i):
            return (b * nq + jnp.clip(i + (d - nside), 0, nq - 1), col0 // HEAD_DIM + h)
        return pl.BlockSpec((tq, HEAD_DIM), imap)

    in_specs = [pl.BlockSpec((tq, HEAD_DIM), lambda h, b, i: (b * nq + i, q_col0 // HEAD_DIM + h))]
    in_specs += [kv_spec(k_col0, d) for d in range(nb)]
    in_specs += [kv_spec(v_col0, d) for d in range(nb)]
    in_specs += [pl.BlockSpec((1, nb, tq, tq), lambda h, b, i: (h, 0, 0, 0))]
    return pl.pallas_call(
        functools.partial(_dilated_bounded_kernel if bounded else _dilated_kernel,
                          nside=nside, nq=nq),
        grid=(B_HEADS, B, nq),
        in_specs=in_specs,
        out_specs=pl.BlockSpec((tq, HEAD_DIM), lambda h, b, i: (b * nq + i, h)),
        out_shape=jax.ShapeDtypeStruct((B * S, B_HEADS * HEAD_DIM), BF16),
        compiler_params=_params("parallel", "parallel", "parallel"),
        name="dilated_bounded" if bounded else "dilated_attention",
    )(*([proj] * (1 + 2 * nb)), table)


def _outproj_kernel(*refs, n_parts):
    a_refs = refs[:n_parts]
    w_refs = refs[n_parts:2 * n_parts]
    r_ref, o_ref = refs[2 * n_parts], refs[2 * n_parts + 1]
    acc = r_ref[...]
    for a_ref, w_ref in zip(a_refs, w_refs):
        acc = acc + jnp.dot(a_ref[...], w_ref[...], preferred_element_type=F32)
    o_ref[...] = acc


def _outproj_residual(parts, w, res, *, tm, tn):
    T, N = res.shape
    n_parts = len(parts)
    in_specs, w_args, off = [], [], 0
    for a in parts:
        kp = a.shape[1]
        in_specs.append(pl.BlockSpec((tm, kp), lambda i, j: (i, 0)))
        w_args.append((kp, off // kp))
        off += kp
    for kp, blk in w_args:
        in_specs.append(pl.BlockSpec((kp, tn), lambda i, j, blk=blk: (blk, j)))
    in_specs.append(pl.BlockSpec((tm, tn), lambda i, j: (i, j)))
    return pl.pallas_call(
        functools.partial(_outproj_kernel, n_parts=n_parts),
        grid=(T // tm, N // tn),
        in_specs=in_specs,
        out_specs=pl.BlockSpec((tm, tn), lambda i, j: (i, j)),
        out_shape=jax.ShapeDtypeStruct((T, N), F32),
        compiler_params=_params("parallel", "arbitrary"),
        name="outproj_residual",
    )(*parts, *([w] * n_parts), res)


def _silu(g):
    return g / (1.0 + jnp.exp(-g))


def _ffn_kernel(x_ref, g_ref, wg_ref, wu_ref, wd_ref, o_ref, xn_ref):
    @pl.when(pl.program_id(1) == 0)
    def _():
        x = x_ref[...]
        xn_ref[...] = (_rms(x) * g_ref[...]).astype(BF16)
        o_ref[...] = x

    xn = xn_ref[...]
    gate = jnp.dot(xn, wg_ref[...], preferred_element_type=F32)
    up = jnp.dot(xn, wu_ref[...], preferred_element_type=F32)
    hid = (_silu(gate) * up).astype(BF16)
    o_ref[...] += jnp.dot(hid, wd_ref[...], preferred_element_type=F32)


def _ffn_residual(x, g, w_gate, w_up, w_down, *, tm, tf):
    T, D = x.shape
    Fd = w_gate.shape[1]
    nf = Fd // tf
    return pl.pallas_call(
        _ffn_kernel,
        grid=(T // tm, nf),
        in_specs=[
            pl.BlockSpec((tm, D), lambda i, f: (i, 0)),
            pl.BlockSpec((1, D), lambda i, f: (0, 0)),
            pl.BlockSpec((D, tf), lambda i, f: (0, f)),
            pl.BlockSpec((D, tf), lambda i, f: (0, f)),
            pl.BlockSpec((tf, D), lambda i, f: (f, 0)),
        ],
        out_specs=pl.BlockSpec((tm, D), lambda i, f: (i, 0)),
        out_shape=jax.ShapeDtypeStruct((T, D), F32),
        scratch_shapes=[pltpu.VMEM((tm, D), BF16)],
        compiler_params=_params("parallel", "arbitrary"),
        name="ffn_residual",
    )(x, g.reshape(1, D), w_gate, w_up, w_down)


def _diff_kernel(slopes_ref, q_ref, k_ref, v_ref, dmat_ref, lq1_ref, lk1_ref, lq2_ref, lk2_ref,
                 subln_ref, o_ref, m_ref, l_ref, acc_ref, *, tq, tk, nk, lambda_init):
    h = pl.program_id(1)
    i = pl.program_id(2)
    j = pl.program_id(3)

    @pl.when(j == 0)
    def _():
        m_ref[...] = jnp.full(m_ref.shape, NEG_INF, F32)
        l_ref[...] = jnp.zeros(l_ref.shape, F32)
        acc_ref[...] = jnp.zeros(acc_ref.shape, F32)

    neg_slope = -slopes_ref[h] * LOG2E
    bias = _alibi_bias(dmat_ref[...], (j * tk - i * tq).astype(F32), neg_slope, tk)
    v = v_ref[...]
    for c in range(2):
        sl = slice(c * HEAD_DIM, (c + 1) * HEAD_DIM)
        s = lax.dot_general(q_ref[:, sl], k_ref[:, sl], _NT, preferred_element_type=F32) + bias
        m_prev = m_ref[c]
        m_new = jnp.maximum(m_prev, jnp.max(s, axis=1, keepdims=True))
        alpha = jnp.exp2(m_prev - m_new)
        p = jnp.exp2(s - m_new[:, :1])
        l_ref[c] = alpha * l_ref[c] + jnp.sum(p, axis=1, keepdims=True)
        acc_ref[c] = acc_ref[c] * alpha[:, :1] + jnp.dot(p.astype(BF16), v,
                                                         preferred_element_type=F32)
        m_ref[c] = m_new

    @pl.when(j == nk - 1)
    def _():
        _diff_finalize(acc_ref, l_ref, lq1_ref, lk1_ref, lq2_ref, lk2_ref, subln_ref, o_ref,
                       lambda_init)


def _alibi_bias(dbase, offset, neg_slope, tk):
    return jnp.concatenate([neg_slope * jnp.abs(dbase + (offset + float(cb * LANES)))
                            for cb in range(tk // LANES)], axis=1)


def _diff_finalize(acc_ref, l_ref, lq1_ref, lk1_ref, lq2_ref, lk2_ref, subln_ref, o_ref, lambda_init):
    lam = (jnp.exp(jnp.sum(lq1_ref[...] * lk1_ref[...], axis=-1, keepdims=True))
           - jnp.exp(jnp.sum(lq2_ref[...] * lk2_ref[...], axis=-1, keepdims=True))
           + lambda_init)
    o = acc_ref[0] / l_ref[0][:, :1] - lam * (acc_ref[1] / l_ref[1][:, :1])
    o = _rms(o) * subln_ref[...] * (1.0 - lambda_init)
    o_ref[...] = o.astype(o_ref.dtype)


def _diff_bounded_kernel(*refs, tq, tk, nk, lambda_init, n_cast):
    (slopes_ref, q_ref, k_ref, v_ref, dmat_ref, lq1_ref, lk1_ref, lq2_ref, lk2_ref,
     subln_ref) = refs[:10]
    cast_in, o_ref = refs[10:10 + n_cast], refs[10 + n_cast]
    cast_out = refs[11 + n_cast:11 + 2 * n_cast]
    l_ref, acc_ref = refs[-2:]
    for src, dst in zip(cast_in, cast_out):
        dst[...] = src[...].astype(dst.dtype)
    h = pl.program_id(1)
    i = pl.program_id(2)
    j = pl.program_id(3)

    @pl.when(j == 0)
    def _():
        l_ref[...] = jnp.zeros(l_ref.shape, F32)
        acc_ref[...] = jnp.zeros(acc_ref.shape, F32)

    neg_slope = -slopes_ref[h] * LOG2E
    q0 = i * tq
    k0 = j * tk
    v = v_ref[...]

    def scores(c):
        sl = slice(c * HEAD_DIM, (c + 1) * HEAD_DIM)
        return lax.dot_general(q_ref[:, sl], k_ref[:, sl], _NT, preferred_element_type=F32)

    overlaps = (k0 < q0 + tq) & (q0 < k0 + tk)

    @pl.when(overlaps)
    def _():
        for c in range(2):
            p = jnp.exp2(scores(c) + _alibi_bias(dmat_ref[...], (k0 - q0).astype(F32), neg_slope, tk))
            l_ref[c] += jnp.sum(p, axis=1, keepdims=True)
            acc_ref[c] += jnp.dot(p.astype(BF16), v, preferred_element_type=F32)

    @pl.when(jnp.logical_not(overlaps))
    def _():
        right = k0 >= q0 + tq
        gap = jnp.where(right, k0 - (q0 + tq), q0 - (k0 + tk)).astype(F32)
        key = lax.broadcasted_iota(jnp.int32, (1, tk), 1)
        qry = lax.broadcasted_iota(jnp.int32, (tq, 1), 0)
        key_bias = neg_slope * jnp.where(right, key, tk - key).astype(F32)
        qry_fac = jnp.exp2(neg_slope * (jnp.where(right, tq - qry, qry).astype(F32) + gap))
        for c in range(2):
            p = jnp.exp2(scores(c) + key_bias)
            l_ref[c] += qry_fac * jnp.sum(p, axis=1, keepdims=True)
            acc_ref[c] += qry_fac * jnp.dot(p.astype(BF16), v, preferred_element_type=F32)

    @pl.when(j == nk - 1)
    def _():
        _diff_finalize(acc_ref, l_ref, lq1_ref, lk1_ref, lq2_ref, lk2_ref, subln_ref, o_ref,
                       lambda_init)


BF16_SUBLANES = 16


def _cast_rows_per_step(n_rows, n_steps):
    for share in (1, 2, 4, 8):
        rows, rem = divmod(n_rows * share, n_steps)
        if rem == 0 and rows % BF16_SUBLANES == 0 and n_steps % share == 0:
            return rows, share
    return None


def _diff_attention(proj, B, S, slopes, lam_vecs, subln, lambda_init, *, tq, tk, bounded,
                    cast_2d=()):
    H = C_HEADS
    hw = 2 * HEAD_DIM
    nq, nk = S // tq, S // tk
    dmat = (jnp.arange(LANES, dtype=F32)[None, :] - jnp.arange(tq, dtype=F32)[:, None])
    stat = [pltpu.VMEM((2, tq, LANES), F32)]
    kw = dict(tq=tq, tk=tk, nk=nk, lambda_init=lambda_init)
    cast_specs = []
    if bounded:
        body, name = _diff_bounded_kernel, "diff_bounded"
        kw["n_cast"] = len(cast_2d)
        for a in cast_2d:
            rows, share = _cast_rows_per_step(a.shape[0], B * H * nq * nk)
            cast_specs.append(pl.BlockSpec(
                (rows, a.shape[1]),
                lambda b, h, i, j, s, share=share: ((((b * H + h) * nq + i) * nk + j) // share, 0)))
    else:
        assert not cast_2d
        body, name, stat = _diff_kernel, "diff_attention", stat * 2
    o_spec = pl.BlockSpec((tq, hw), lambda b, h, i, j, s: (b * nq + i, h))
    o_shape = jax.ShapeDtypeStruct((B * S, H * hw), BF16)
    vec_spec = pl.BlockSpec((1, HEAD_DIM), lambda b, h, i, j, s: (0, 0))
    grid_spec = pltpu.PrefetchScalarGridSpec(
        num_scalar_prefetch=1,
        grid=(B, H, nq, nk),
        in_specs=[
            pl.BlockSpec((tq, hw), lambda b, h, i, j, s: (b * nq + i, h)),
            pl.BlockSpec((tk, hw), lambda b, h, i, j, s: (b * nk + j, H + h)),
            pl.BlockSpec((tk, hw), lambda b, h, i, j, s: (b * nk + j, 2 * H + h)),
            pl.BlockSpec((tq, LANES), lambda b, h, i, j, s: (0, 0)),
            vec_spec, vec_spec, vec_spec, vec_spec,
            pl.BlockSpec((1, hw), lambda b, h, i, j, s: (0, 0)),
        ] + cast_specs,
        out_specs=[o_spec] + cast_specs if cast_specs else o_spec,
        scratch_shapes=stat + [pltpu.VMEM((2, tq, hw), F32)],
    )
    cast_shapes = [jax.ShapeDtypeStruct(a.shape, BF16) for a in cast_2d]
    return pl.pallas_call(
        functools.partial(body, **kw),
        grid_spec=grid_spec,
        out_shape=[o_shape] + cast_shapes if cast_specs else o_shape,
        compiler_params=_params("arbitrary", "arbitrary", "arbitrary", "arbitrary"),
        name=name,
    )(slopes, proj, proj, proj, dmat, *[v.reshape(1, HEAD_DIM) for v in lam_vecs],
      subln.reshape(1, hw), *cast_2d)


def _router_kernel(x_ref, g_ref, wr_ref, meta_ref, cnt_ref, base_ref, *, tm, n_exp):
    i = pl.program_id(0)

    @pl.when(i == 0)
    def _():
        base_ref[...] = jnp.zeros(base_ref.shape, F32)

    hn = _rms(x_ref[...]) * g_ref[...]
    logits = jnp.dot(hn, wr_ref[...], preferred_element_type=F32,
                     precision=lax.Precision.HIGHEST)
    lane = lax.broadcasted_iota(jnp.int32, (tm, LANES), 1)
    logits = jnp.where(lane < n_exp, logits, NEG_INF)
    t1 = jnp.max(logits, axis=1, keepdims=True)
    i1 = jnp.min(jnp.where(logits == t1, lane, LANES), axis=1, keepdims=True)
    rest = jnp.where(lane == i1, NEG_INF, logits)
    t2 = jnp.max(rest, axis=1, keepdims=True)
    i2 = jnp.min(jnp.where(rest == t2, lane, LANES), axis=1, keepdims=True)
    ex = jnp.exp(t2 - t1)
    g1 = 1.0 / (1.0 + ex)
    g2 = ex * g1
    oh1 = lane == i1
    oh2 = lane == i2
    member = jnp.where(oh1 | oh2, 1.0, 0.0)
    row = lax.broadcasted_iota(jnp.int32, (tm, tm), 0)
    col = lax.broadcasted_iota(jnp.int32, (tm, tm), 1)
    strict_lower = jnp.where(col < row, 1.0, 0.0).astype(BF16)
    before = jnp.dot(strict_lower, member.astype(BF16), preferred_element_type=F32) + base_ref[...]
    r1 = jnp.sum(jnp.where(oh1, before, 0.0), axis=1, keepdims=True)
    r2 = jnp.sum(jnp.where(oh2, before, 0.0), axis=1, keepdims=True)
    base_ref[...] += jnp.sum(member, axis=0, keepdims=True)
    meta = jnp.where(lane == 0, i1.astype(F32), 0.0)
    meta = jnp.where(lane == 1, i2.astype(F32), meta)
    meta = jnp.where(lane == 2, g1, meta)
    meta = jnp.where(lane == 3, g2, meta)
    meta = jnp.where(lane == 4, r1, meta)
    meta = jnp.where(lane == 5, r2, meta)
    meta_ref[...] = meta
    cnt_ref[...] = base_ref[...]


def _router(x, g, w_router, *, tm):
    T, D = x.shape
    n_exp = w_router.shape[1]
    wr = jnp.zeros((D, LANES), F32).at[:, :n_exp].set(w_router)
    return pl.pallas_call(
        functools.partial(_router_kernel, tm=tm, n_exp=n_exp),
        grid=(T // tm,),
        in_specs=[
            pl.BlockSpec((tm, D), lambda i: (i, 0)),
            pl.BlockSpec((1, D), lambda i: (0, 0)),
            pl.BlockSpec((D, LANES), lambda i: (0, 0)),
        ],
        out_specs=[pl.BlockSpec((tm, LANES), lambda i: (i, 0)),
                   pl.BlockSpec((1, LANES), lambda i: (0, 0))],
        out_shape=[jax.ShapeDtypeStruct((T, LANES), F32),
                   jax.ShapeDtypeStruct((1, LANES), F32)],
        scratch_shapes=[pltpu.VMEM((1, LANES), F32)],
        compiler_params=_params("arbitrary"),
        name="moe_router",
    )(x, g.reshape(1, D), wr)


ISSUE_UNROLL = 8


def _row_copy(src_ref, src_row, dst_ref, dst_row, sem):
    return pltpu.make_async_copy(src_ref.at[pl.ds(src_row, 1)], dst_ref.at[pl.ds(dst_row, 1)], sem)


def _dispatch_kernel(pend_ref, dest_hbm, x_ref, xs_hbm, idx_ref, zero_ref, idx_sem, row_sem, zero_sem,
                     *, tm, bm, n_rows):
    i = pl.program_id(0)

    @pl.when(i == 0)
    def _():
        zero_ref[...] = jnp.zeros(zero_ref.shape, F32)
        fills = [pltpu.make_async_copy(zero_ref, xs_hbm.at[pl.ds(n_rows - (e + 1) * bm, bm)], zero_sem)
                 for e in range(N_EXPERTS)]
        for e in range(N_EXPERTS):
            start = pl.multiple_of(jnp.maximum(pend_ref[e] - bm, 0), 8)
            fills.append(pltpu.make_async_copy(zero_ref, xs_hbm.at[pl.ds(start, bm)], zero_sem))
        for f in fills:
            f.start()
            f.wait()

    n_idx = TOP_K * tm
    idx_copy = pltpu.make_async_copy(dest_hbm.at[pl.ds(i * n_idx, n_idx)], idx_ref, idx_sem)
    idx_copy.start()
    idx_copy.wait()

    def issue(t, carry):
        for k in range(TOP_K):
            _row_copy(x_ref, t, xs_hbm, idx_ref[TOP_K * t + k], row_sem).start()
        return carry

    lax.fori_loop(0, tm, issue, 0, unroll=ISSUE_UNROLL)

    for k in range(TOP_K):
        pltpu.make_async_copy(x_ref, xs_hbm.at[pl.ds(0, tm)], row_sem).wait()


def _dispatch(dest, x, pend, n_rows, *, tm, bm):
    T, D = x.shape
    grid_spec = pltpu.PrefetchScalarGridSpec(
        num_scalar_prefetch=1,
        grid=(T // tm,),
        in_specs=[pl.BlockSpec(memory_space=pl.ANY),
                  pl.BlockSpec((tm, D), lambda i, pend: (i, 0))],
        out_specs=pl.BlockSpec(memory_space=pl.ANY),
        scratch_shapes=[pltpu.SMEM((TOP_K * tm,), jnp.int32), pltpu.VMEM((bm, D), F32),
                        pltpu.SemaphoreType.DMA(()), pltpu.SemaphoreType.DMA(()),
                        pltpu.SemaphoreType.DMA(())],
    )
    return pl.pallas_call(
        functools.partial(_dispatch_kernel, tm=tm, bm=bm, n_rows=n_rows),
        grid_spec=grid_spec,
        out_shape=jax.ShapeDtypeStruct((n_rows, D), F32),
        compiler_params=pltpu.CompilerParams(dimension_semantics=("arbitrary",),
                                             vmem_limit_bytes=VMEM_LIMIT_BYTES,
                                             has_side_effects=True),
        name="moe_dispatch",
    )(pend.astype(jnp.int32), dest, x)


def _expert_kernel(be_ref, bn_ref, x_ref, g_ref, wg_ref, wu_ref, wd_ref, y_ref, xn_ref, *, bm):
    del be_ref
    n_valid = bn_ref[pl.program_id(0)]

    @pl.when(pl.program_id(1) == 0)
    def _():
        rows = lax.broadcasted_iota(jnp.int32, (bm, 1), 0)
        x = jnp.where(rows < n_valid, x_ref[...], 0.0)
        xn_ref[...] = (_rms(x) * g_ref[...]).astype(BF16)
        y_ref[...] = jnp.zeros(y_ref.shape, F32)

    def ffn_rows(rs):
        xn = xn_ref[rs, :]
        gate = jnp.dot(xn, wg_ref[...], preferred_element_type=F32)
        up = jnp.dot(xn, wu_ref[...], preferred_element_type=F32)
        hid = (_silu(gate) * up).astype(BF16)
        y_ref[rs, :] += jnp.dot(hid, wd_ref[...], preferred_element_type=F32)

    half = bm // 2

    @pl.when(n_valid > half)
    def _():
        ffn_rows(slice(0, bm))

    @pl.when((n_valid > 0) & (n_valid <= half))
    def _():
        ffn_rows(slice(0, half))


def _expert_ffn(xs, g, e_gate, e_up, e_down, blk_expert, blk_valid, *, bm, tf):
    P, D = xs.shape
    Fd = e_gate.shape[2]
    nf = Fd // tf
    nblk = P // bm

    def f_idx(b, f, bn):
        return jnp.where(bn[b] > 0, f, nf - 1)

    grid_spec = pltpu.PrefetchScalarGridSpec(
        num_scalar_prefetch=2,
        grid=(nblk, nf),
        in_specs=[
            pl.BlockSpec((bm, D), lambda b, f, be, bn: (b, 0)),
            pl.BlockSpec((1, D), lambda b, f, be, bn: (0, 0)),
            pl.BlockSpec((None, D, tf), lambda b, f, be, bn: (be[b], 0, f_idx(b, f, bn))),
            pl.BlockSpec((None, D, tf), lambda b, f, be, bn: (be[b], 0, f_idx(b, f, bn))),
            pl.BlockSpec((None, tf, D), lambda b, f, be, bn: (be[b], f_idx(b, f, bn), 0)),
        ],
        out_specs=pl.BlockSpec((bm, D), lambda b, f, be, bn: (b, 0)),
        scratch_shapes=[pltpu.VMEM((bm, D), BF16)],
    )
    return pl.pallas_call(
        functools.partial(_expert_kernel, bm=bm),
        grid_spec=grid_spec,
        out_shape=jax.ShapeDtypeStruct((P, D), F32),
        compiler_params=_params("arbitrary", "arbitrary"),
        name="moe_experts",
    )(blk_expert, blk_valid, xs, g.reshape(1, D), e_gate, e_up, e_down)


def _combine_kernel(dest_hbm, ys_hbm, x_ref, meta_ref, o_ref, idx_ref, ybuf_ref, idx_sem, row_sem,
                    *, tm):
    i = pl.program_id(0)
    n_idx = TOP_K * tm
    idx_copy = pltpu.make_async_copy(dest_hbm.at[pl.ds(i * n_idx, n_idx)], idx_ref, idx_sem)
    idx_copy.start()
    idx_copy.wait()

    def issue(t, carry):
        for k in range(TOP_K):
            _row_copy(ys_hbm, idx_ref[TOP_K * t + k], ybuf_ref.at[k], t, row_sem).start()
        return carry

    lax.fori_loop(0, tm, issue, 0, unroll=ISSUE_UNROLL)

    for k in range(TOP_K):
        pltpu.make_async_copy(ys_hbm.at[pl.ds(0, tm)], ybuf_ref.at[k], row_sem).wait()

    meta = meta_ref[...]
    out = x_ref[...]
    for k in range(TOP_K):
        out = out + meta[:, 2 + k:3 + k] * ybuf_ref[k]
    o_ref[...] = out


def _combine(dest, ys, x, meta, *, tm):
    T, D = x.shape
    return pl.pallas_call(
        functools.partial(_combine_kernel, tm=tm),
        grid=(T // tm,),
        in_specs=[pl.BlockSpec(memory_space=pl.ANY), pl.BlockSpec(memory_space=pl.ANY),
                  pl.BlockSpec((tm, D), lambda i: (i, 0)),
                  pl.BlockSpec((tm, LANES), lambda i: (i, 0))],
        out_specs=pl.BlockSpec((tm, D), lambda i: (i, 0)),
        out_shape=jax.ShapeDtypeStruct((T, D), F32),
        scratch_shapes=[pltpu.SMEM((TOP_K * tm,), jnp.int32),
                        pltpu.VMEM((TOP_K, tm, D), F32),
                        pltpu.SemaphoreType.DMA(()), pltpu.SemaphoreType.DMA(())],
        compiler_params=_params("arbitrary"),
        name="moe_combine",
    )(dest, ys, x, meta)


def _moe_residual(x, g, w_router, e_gate, e_up, e_down, *, tm_route, tm_move, bm, tf):
    T, D = x.shape
    meta, counts = _router(x, g, w_router, tm=tm_route)
    counts = counts[0, :N_EXPERTS].astype(jnp.int32)
    padded = (counts + bm - 1) // bm * bm
    pend = jnp.cumsum(padded)
    pstart = pend - padded
    expert = meta[:, 0:TOP_K].astype(jnp.int32)
    rank = meta[:, 4:4 + TOP_K].astype(jnp.int32)
    dest = (pstart[expert] + rank).reshape(T * TOP_K)
    n_rows = T * TOP_K + N_EXPERTS * bm
    nblk = n_rows // bm
    blk_row0 = jnp.arange(nblk, dtype=jnp.int32) * bm
    blk_expert = jnp.minimum(jnp.searchsorted(pend, blk_row0, side='right'),
                             N_EXPERTS - 1).astype(jnp.int32)
    blk_valid = jnp.clip(counts[blk_expert] - (blk_row0 - pstart[blk_expert]), 0, bm)
    blk_valid = jnp.where(blk_row0 < pend[-1], blk_valid, 0).astype(jnp.int32)
    last_used = jnp.maximum(pend[-1] // bm - 1, 0)
    blk_expert = jnp.where(blk_row0 < pend[-1], blk_expert, blk_expert[last_used])

    xs = _dispatch(dest, x, pend, n_rows, tm=tm_move, bm=bm)
    ys = _expert_ffn(xs, g, e_gate, e_up, e_down, blk_expert, blk_valid, bm=bm, tf=tf)
    return _combine(dest, ys, x, meta, tm=tm_move)


def _alibi_slopes(n_heads):
    return 2.0 ** (-8.0 * jnp.arange(1, n_heads + 1, dtype=F32) / n_heads)


def _tile(n, want):
    t = min(n, want)
    while n % t:
        t //= 2
    return t


Q_SCALE = HEAD_DIM ** -0.5 * LOG2E
PROJ_TN = 512
BOUNDED_SCORE_LIMIT = 60.0


class _Tiles(NamedTuple):
    rows: int
    out_cols: int
    attn_q: int
    attn_k: int
    diff_k: int
    online_q: int
    online_k: int
    band_q: int
    ffn_rows: int
    ffn_hidden: int
    moe_rows: int
    moe_hidden: int


def _tiles(S, T, d_ff_expert=1024):
    return _Tiles(rows=_tile(S, 1024), out_cols=1024,
                  attn_q=_tile(S, 1024), attn_k=_tile(S, 4096), diff_k=_tile(S, 8192),
                  online_q=_tile(S, 512), online_k=_tile(S, 1024), band_q=_tile(S, 512),
                  ffn_rows=_tile(T, 512), ffn_hidden=512,
                  moe_rows=_tile(T, 512), moe_hidden=_tile(d_ff_expert, 1024))


def _score_bound(q_gain, k_gain):
    return 1.01 * HEAD_DIM * Q_SCALE * jnp.max(jnp.abs(q_gain)) * jnp.max(jnp.abs(k_gain))


def _layer0(x, norm_mix, w_in, qn_a, kn_a, qn_b, kn_b, w_out, norm_ffn, w_gate, w_up, w_down):
    B, S, D = x.shape
    T = B * S
    xf = x.reshape(T, D)
    t = _tiles(S, T)
    a_q, a_kv, b_dim = A_HEADS * HEAD_DIM, A_KV_HEADS * HEAD_DIM, B_HEADS * HEAD_DIM
    w_in = w_in.astype(BF16)
    ops_a = ([(0, True, Q_SCALE)] * A_HEADS + [(1, True, 1.0)] * A_KV_HEADS
             + [(None, False, 1.0)] * A_KV_HEADS)
    ops_b = [(0, False, Q_SCALE)] * B_HEADS + [(1, False, 1.0)] * B_HEADS + [(None, False, 1.0)] * B_HEADS
    proj_a = _normproj(xf, norm_mix, w_in, ops_a, [qn_a, kn_a], _rope_tables(S), col0=0,
                       tm=t.rows, tn=PROJ_TN)
    proj_b = _normproj(xf, norm_mix, w_in, ops_b, [qn_b, kn_b], None, col0=a_q + 2 * a_kv,
                       tm=t.rows, tn=PROJ_TN)
    gqa = functools.partial(_gqa_attention, B=B, S=S, q_col0=0, k_col0=a_q, v_col0=a_q + a_kv)
    o_a = lax.cond(_score_bound(qn_a, kn_a) <= BOUNDED_SCORE_LIMIT,
                   lambda p: gqa(p, tq=t.attn_q, tk=t.attn_k, bounded=True),
                   lambda p: gqa(p, tq=t.online_q, tk=t.online_k, bounded=False), proj_a)
    dil = functools.partial(_dilated_attention, B=B, S=S, slopes=_alibi_slopes(B_HEADS), q_col0=0,
                            k_col0=b_dim, v_col0=2 * b_dim, tq=t.band_q)
    o_b = lax.cond(_score_bound(qn_b, kn_b) <= BOUNDED_SCORE_LIMIT,
                   lambda p: dil(p, bounded=True), lambda p: dil(p, bounded=False), proj_b)
    x1 = _outproj_residual([o_a, o_b], w_out.astype(BF16), xf, tm=t.rows, tn=t.out_cols)
    x2 = _ffn_residual(x1, norm_ffn, w_gate.astype(BF16), w_up.astype(BF16), w_down.astype(BF16),
                       tm=t.ffn_rows, tf=t.ffn_hidden)
    return x2.reshape(B, S, D)


def _layer1(x, norm_mix, w_in, qn_c, kn_c, lam_q1, lam_k1, lam_q2, lam_k2, subln, w_out, norm_ffn,
            w_router, e_gate, e_up, e_down):
    B, S, D = x.shape
    T = B * S
    xf = x.reshape(T, D)
    t = _tiles(S, T, e_gate.shape[2])
    head_ops = ([(0, False, Q_SCALE)] * (2 * C_HEADS) + [(1, False, 1.0)] * (2 * C_HEADS)
                + [(None, False, 1.0)] * (2 * C_HEADS))
    proj = _normproj(xf, norm_mix, w_in.astype(BF16), head_ops, [qn_c, kn_c], None, col0=0,
                     tm=t.rows, tn=2 * PROJ_TN)
    lambda_init = 0.8 - 0.6 * math.exp(-0.3 * 1)
    diff = functools.partial(_diff_attention, B=B, S=S, slopes=_alibi_slopes(C_HEADS),
                             lam_vecs=(lam_q1, lam_k1, lam_q2, lam_k2), subln=subln,
                             lambda_init=lambda_init)
    tq, tk = t.attn_q, t.diff_k
    n_steps = B * C_HEADS * (S // tq) * (S // tk)
    rows_of = lambda w: math.prod(w.shape[:-1])
    stream_casts = all(_cast_rows_per_step(rows_of(w), n_steps) for w in (e_gate, e_up, e_down))

    def bounded_branch(p, *ws):
        if not stream_casts:
            return (diff(p, tq=tq, tk=tk, bounded=True), *[w.astype(BF16) for w in ws])
        o, *cast = diff(p, tq=tq, tk=tk, bounded=True,
                        cast_2d=[w.reshape(-1, w.shape[-1]) for w in ws])
        return (o, *[c.reshape(w.shape) for c, w in zip(cast, ws)])

    def online_branch(p, *ws):
        return (diff(p, tq=t.online_q, tk=t.online_k, bounded=False),
                *[w.astype(BF16) for w in ws])

    o_c, eg, eu, ed = lax.cond(_score_bound(qn_c, kn_c) <= BOUNDED_SCORE_LIMIT,
                               bounded_branch, online_branch, proj, e_gate, e_up, e_down)
    x3 = _outproj_residual([o_c], w_out.astype(BF16), xf, tm=t.rows, tn=t.out_cols)
    out = _moe_residual(x3, norm_ffn, w_router, eg, eu, ed, tm_route=t.moe_rows,
                        tm_move=t.moe_rows, bm=t.moe_rows, tf=t.moe_hidden)
    return out.reshape(B, S, D)


def kernel(x, l0_norm_mix,l0_w_in, l0_qnorm_a, l0_knorm_a, l0_qnorm_b, l0_knorm_b, l0_w_out, l0_norm_ffn, l0_w_gate, l0_w_up, l0_w_down, l1_norm_mix, l1_w_in, l1_qnorm_c, l1_knorm_c, l1_lambda_q1, l1_lambda_k1, l1_lambda_q2, l1_lambda_k2, l1_subln, l1_w_out, l1_norm_ffn, l1_w_router, l1_e_gate, l1_e_up, l1_e_down):
    x = _layer0(x, l0_norm_mix, l0_w_in, l0_qnorm_a, l0_knorm_a, l0_qnorm_b, l0_knorm_b, l0_w_out,
                l0_norm_ffn, l0_w_gate, l0_w_up, l0_w_down)
    return _layer1(x, l1_norm_mix, l1_w_in, l1_qnorm_c, l1_knorm_c, l1_lambda_q1, l1_lambda_k1,
                   l1_lambda_q2, l1_lambda_k2, l1_subln, l1_w_out, l1_norm_ffn, l1_w_router,
                   l1_e_gate, l1_e_up, l1_e_down)
```

```python
import functools
import math
from typing import NamedTuple

import jax
import jax.numpy as jnp
from jax import lax
from jax.experimental import pallas as pl
from jax.experimental.pallas import tpu as pltpu

F32 = jnp.float32
BF16 = jnp.bfloat16

HEAD_DIM = 128
LANES = 128
A_HEADS = 8
A_KV_HEADS = 2
B_HEADS = 8
B_PATTERNS = ((128, 1), (512, 4), (2048, 16))
C_HEADS = 8
N_EXPERTS = 8
TOP_K = 2
GRID_W = 64
ROPE_THETA = 10000.0
NORM_EPS = 1e-6
NEG_INF = -1e30
LOG2E = 1.4426950408889634
VMEM_LIMIT_BYTES = 56 * 1024 * 1024

_NT = (((1,), (1,)), ((), ()))


def _params(*sem):
    return pltpu.CompilerParams(dimension_semantics=sem, vmem_limit_bytes=VMEM_LIMIT_BYTES)


def _rms(x, eps=NORM_EPS):
    return x * lax.rsqrt(jnp.mean(x * x, axis=-1, keepdims=True) + eps)


PROJ_ROW_CHUNKS = 2


def _normproj_kernel(*refs, heads_per_tile, has_rope):
    if has_rope:
        x_ref, g_ref, w_ref, cols_ref, cos_ref, sa_ref, sb_ref, o_ref, xn_ref = refs
    else:
        x_ref, g_ref, w_ref, cols_ref, o_ref, xn_ref = refs

    @pl.when(pl.program_id(1) == 0)
    def _():
        xn_ref[...] = (_rms(x_ref[...]) * g_ref[...]).astype(BF16)

    rows = x_ref.shape[0] // PROJ_ROW_CHUNKS
    for r in range(PROJ_ROW_CHUNKS):
        rs = slice(r * rows, (r + 1) * rows)
        acc = jnp.dot(xn_ref[rs, :], w_ref[...], preferred_element_type=F32)
        for h in range(heads_per_tile):
            sl = slice(h * HEAD_DIM, (h + 1) * HEAD_DIM)
            y = acc[:, sl]
            inv = lax.rsqrt(jnp.mean(y * y, axis=-1, keepdims=True) + NORM_EPS)
            y = y * (inv * cols_ref[0:1, sl] + cols_ref[1:2, sl])
            if has_rope:
                rot = (y * cos_ref[rs, :] + pltpu.roll(y, 96, 1) * sa_ref[rs, :]
                       + pltpu.roll(y, 32, 1) * sb_ref[rs, :])
                y = y + cols_ref[2:3, sl] * (rot - y)
            o_ref[rs, sl] = (y * cols_ref[3:4, sl]).astype(o_ref.dtype)


def _normproj(x, g, w, head_ops, gains, rope_tables, *, col0, tm, tn):
    T, D = x.shape
    N = len(head_ops) * HEAD_DIM
    has_rope = rope_tables is not None
    zero, one = jnp.zeros((HEAD_DIM,), F32), jnp.ones((HEAD_DIM,), F32)
    cols = jnp.stack([
        jnp.concatenate([zero if gi is None else gains[gi] for gi, _, _ in head_ops]),
        jnp.concatenate([one if gi is None else zero for gi, _, _ in head_ops]),
        jnp.concatenate([one if rope else zero for _, rope, _ in head_ops]),
        jnp.concatenate([one * scale for _, _, scale in head_ops])])
    cols = jnp.concatenate([cols, jnp.zeros((4, N), F32)])
    jt0 = col0 // tn
    in_specs = [
        pl.BlockSpec((tm, D), lambda i, j: (i, 0)),
        pl.BlockSpec((1, D), lambda i, j: (0, 0)),
        pl.BlockSpec((D, tn), lambda i, j: (0, jt0 + j)),
        pl.BlockSpec((8, tn), lambda i, j: (0, j)),
    ]
    args = [x, g.reshape(1, D), w, cols]
    if has_rope:
        ns = rope_tables[0].shape[0] // tm
        for t in rope_tables:
            in_specs.append(pl.BlockSpec((tm, HEAD_DIM), lambda i, j: (i % ns, 0)))
            args.append(t)
    return pl.pallas_call(
        functools.partial(_normproj_kernel, heads_per_tile=tn // HEAD_DIM, has_rope=has_rope),
        grid=(T // tm, N // tn),
        in_specs=in_specs,
        out_specs=pl.BlockSpec((tm, tn), lambda i, j: (i, j)),
        out_shape=jax.ShapeDtypeStruct((T, N), BF16),
        scratch_shapes=[pltpu.VMEM((tm, D), BF16)],
        compiler_params=_params("parallel", "arbitrary"),
        name="normproj_rope" if has_rope else "normproj",
    )(*args)


def _rope_tables(S):
    pos = jnp.arange(S, dtype=jnp.int32)
    row = (pos // GRID_W).astype(F32)
    col = (pos % GRID_W).astype(F32)
    nf = HEAD_DIM // 4
    inv = ROPE_THETA ** (-jnp.arange(nf, dtype=F32) / nf)
    ang_row = row[:, None] * inv
    ang_col = col[:, None] * inv
    ang = jnp.concatenate([ang_row, ang_row, ang_col, ang_col], axis=-1)
    cos = jnp.cos(ang)
    sin = jnp.sin(ang)
    quarter = (jnp.arange(HEAD_DIM) // nf) % 2
    sa = jnp.where(quarter[None, :] == 0, -sin, 0.0)
    sb = jnp.where(quarter[None, :] == 1, sin, 0.0)
    return cos, sa, sb


def _gqa_kernel(q_ref, k_ref, v_ref, o_ref, m_ref, l_ref, acc_ref, *, group, nk):
    j = pl.program_id(3)

    @pl.when(j == 0)
    def _():
        m_ref[...] = jnp.full(m_ref.shape, NEG_INF, F32)
        l_ref[...] = jnp.zeros(l_ref.shape, F32)
        acc_ref[...] = jnp.zeros(acc_ref.shape, F32)

    k = k_ref[...]
    v = v_ref[...]
    for h in range(group):
        q = q_ref[:, h * HEAD_DIM:(h + 1) * HEAD_DIM]
        s = lax.dot_general(q, k, _NT, preferred_element_type=F32)
        m_prev = m_ref[h]
        m_new = jnp.maximum(m_prev, jnp.max(s, axis=1, keepdims=True))
        alpha = jnp.exp2(m_prev - m_new)
        p = jnp.exp2(s - m_new[:, :1])
        l_ref[h] = alpha * l_ref[h] + jnp.sum(p, axis=1, keepdims=True)
        acc_ref[h] = acc_ref[h] * alpha + jnp.dot(p.astype(BF16), v, preferred_element_type=F32)
        m_ref[h] = m_new

    @pl.when(j == nk - 1)
    def _():
        for h in range(group):
            o_ref[:, h * HEAD_DIM:(h + 1) * HEAD_DIM] = (acc_ref[h] / l_ref[h]).astype(o_ref.dtype)


def _gqa_bounded_kernel(q_ref, k_ref, v_ref, o_ref, l_ref, acc_ref, *, group, nk):
    j = pl.program_id(3)

    @pl.when(j == 0)
    def _():
        l_ref[...] = jnp.zeros(l_ref.shape, F32)
        acc_ref[...] = jnp.zeros(acc_ref.shape, F32)

    k = k_ref[...]
    v = v_ref[...]
    for h in range(group):
        q = q_ref[:, h * HEAD_DIM:(h + 1) * HEAD_DIM]
        p = jnp.exp2(lax.dot_general(q, k, _NT, preferred_element_type=F32))
        l_ref[h] += jnp.sum(p, axis=1, keepdims=True)
        acc_ref[h] += jnp.dot(p.astype(BF16), v, preferred_element_type=F32)

    @pl.when(j == nk - 1)
    def _():
        for h in range(group):
            o_ref[:, h * HEAD_DIM:(h + 1) * HEAD_DIM] = (acc_ref[h] / l_ref[h]).astype(o_ref.dtype)


def _gqa_attention(proj, B, S, *, q_col0, k_col0, v_col0, tq, tk, bounded):
    group = A_HEADS // A_KV_HEADS
    nq, nk = S // tq, S // tk
    gw = group * HEAD_DIM
    if bounded:
        body = functools.partial(_gqa_bounded_kernel, group=group, nk=nk)
        scratch = [pltpu.VMEM((group, tq, LANES), F32), pltpu.VMEM((group, tq, HEAD_DIM), F32)]
    else:
        body = functools.partial(_gqa_kernel, group=group, nk=nk)
        scratch = [pltpu.VMEM((group, tq, LANES), F32), pltpu.VMEM((group, tq, LANES), F32),
                   pltpu.VMEM((group, tq, HEAD_DIM), F32)]
    return pl.pallas_call(
        body,
        grid=(B, A_KV_HEADS, nq, nk),
        in_specs=[
            pl.BlockSpec((tq, gw), lambda b, g, i, j: (b * nq + i, q_col0 // gw + g)),
            pl.BlockSpec((tk, HEAD_DIM), lambda b, g, i, j: (b * nk + j, k_col0 // HEAD_DIM + g)),
            pl.BlockSpec((tk, HEAD_DIM), lambda b, g, i, j: (b * nk + j, v_col0 // HEAD_DIM + g)),
        ],
        out_specs=pl.BlockSpec((tq, gw), lambda b, g, i, j: (b * nq + i, g)),
        out_shape=jax.ShapeDtypeStruct((B * S, A_HEADS * HEAD_DIM), BF16),
        scratch_shapes=scratch,
        compiler_params=_params("parallel", "parallel", "parallel", "arbitrary"),
        name="gqa_bounded" if bounded else "gqa_attention",
    )(proj, proj, proj)


def _dilated_kernel(q_ref, *refs, nside, nq):
    nb = 2 * nside + 1
    k_refs, v_refs = refs[:nb], refs[nb:2 * nb]
    tbl_ref, o_ref = refs[2 * nb], refs[2 * nb + 1]
    i = pl.program_id(2)
    q = q_ref[...]
    scores = []
    m = None
    for d in range(nb):
        blk = i + (d - nside)
        in_range = (blk >= 0) & (blk < nq)
        s = lax.dot_general(q, k_refs[d][...], _NT, preferred_element_type=F32) + tbl_ref[0, d]
        s = jnp.where(in_range, s, NEG_INF)
        scores.append(s)
        md = jnp.max(s, axis=1, keepdims=True)
        m = md if m is None else jnp.maximum(m, md)
    l = None
    acc = None
    for d in range(nb):
        p = jnp.exp2(scores[d] - m)
        ld = jnp.sum(p, axis=1, keepdims=True)
        ad = jnp.dot(p.astype(BF16), v_refs[d][...], preferred_element_type=F32)
        l = ld if l is None else l + ld
        acc = ad if acc is None else acc + ad
    o_ref[...] = (acc / l).astype(o_ref.dtype)


def _dilated_bounded_kernel(q_ref, *refs, nside, nq):
    nb = 2 * nside + 1
    k_refs, v_refs = refs[:nb], refs[nb:2 * nb]
    tbl_ref, o_ref = refs[2 * nb], refs[2 * nb + 1]
    i = pl.program_id(2)
    q = q_ref[...]
    l = None
    acc = None
    for d in range(nb):
        blk = i + (d - nside)
        in_range = (blk >= 0) & (blk < nq)
        s = lax.dot_general(q, k_refs[d][...], _NT, preferred_element_type=F32) + tbl_ref[0, d]
        p = jnp.exp2(jnp.where(in_range, s, NEG_INF))
        ld = jnp.sum(p, axis=1, keepdims=True)
        ad = jnp.dot(p.astype(BF16), v_refs[d][...], preferred_element_type=F32)
        l = ld if l is None else l + ld
        acc = ad if acc is None else acc + ad
    o_ref[...] = (acc / l).astype(o_ref.dtype)


def _dilated_bias_table(slopes, tq, nside):
    nb = 2 * nside + 1
    a = jnp.arange(tq, dtype=jnp.int32)
    d = ((jnp.arange(nb, dtype=jnp.int32)[:, None, None] - nside) * tq
         + a[None, None, :] - a[None, :, None])
    ad = jnp.abs(d)
    count = jnp.zeros(d.shape, F32)
    for window, dil in B_PATTERNS:
        reach = (window // (2 * dil)) * dil
        count = count + ((ad % dil == 0) & (ad <= reach)).astype(F32)
    logc = jnp.where(count > 0, jnp.log2(jnp.maximum(count, 1.0)), NEG_INF)
    bias = -(slopes * LOG2E)[:, None, None, None] * ad.astype(F32)[None]
    return jnp.where(count[None] > 0, bias + logc[None], NEG_INF)


def _dilated_attention(proj, B, S, slopes, *, q_col0, k_col0, v_col0, tq, bounded):
    reach = max((w // (2 * dl)) * dl for w, dl in B_PATTERNS)
    nside = -(-reach // tq)
    nb = 2 * nside + 1
    nq = S // tq
    table = _dilated_bias_table(slopes, tq, nside)

    def kv_spec(col0, d):
        def imap(h, b, i):
            return (b * nq + jnp.clip(i + (d - nside), 0, nq - 1), col0 // HEAD_DIM + h)
        return pl.BlockSpec((tq, HEAD_DIM), imap)

    in_specs = [pl.BlockSpec((tq, HEAD_DIM), lambda h, b, i: (b * nq + i, q_col0 // HEAD_DIM + h))]
    in_specs += [kv_spec(k_col0, d) for d in range(nb)]
    in_specs += [kv_spec(v_col0, d) for d in range(nb)]
    in_specs += [pl.BlockSpec((1, nb, tq, tq), lambda h, b, i: (h, 0, 0, 0))]
    return pl.pallas_call(
        functools.partial(_dilated_bounded_kernel if bounded else _dilated_kernel,
                          nside=nside, nq=nq),
        grid=(B_HEADS, B, nq),
        in_specs=in_specs,
        out_specs=pl.BlockSpec((tq, HEAD_DIM), lambda h, b, i: (b * nq + i, h)),
        out_shape=jax.ShapeDtypeStruct((B * S, B_HEADS * HEAD_DIM), BF16),
        compiler_params=_params("parallel", "parallel", "parallel"),
        name="dilated_bounded" if bounded else "dilated_attention",
    )(*([proj] * (1 + 2 * nb)), table)


def _outproj_kernel(*refs, n_parts):
    a_refs = refs[:n_parts]
    w_refs = refs[n_parts:2 * n_parts]
    r_ref, o_ref = refs[2 * n_parts], refs[2 * n_parts + 1]
    acc = r_ref[...]
    for a_ref, w_ref in zip(a_refs, w_refs):
        acc = acc + jnp.dot(a_ref[...], w_ref[...], preferred_element_type=F32)
    o_ref[...] = acc


def _outproj_residual(parts, w, res, *, tm, tn):
    T, N = res.shape
    n_parts = len(parts)
    in_specs, w_args, off = [], [], 0
    for a in parts:
        kp = a.shape[1]
        in_specs.append(pl.BlockSpec((tm, kp), lambda i, j: (i, 0)))
        w_args.append((kp, off // kp))
        off += kp
    for kp, blk in w_args:
        in_specs.append(pl.BlockSpec((kp, tn), lambda i, j, blk=blk: (blk, j)))
    in_specs.append(pl.BlockSpec((tm, tn), lambda i, j: (i, j)))
    return pl.pallas_call(
        functools.partial(_outproj_kernel, n_parts=n_parts),
        grid=(T // tm, N // tn),
        in_specs=in_specs,
        out_specs=pl.BlockSpec((tm, tn), lambda i, j: (i, j)),
        out_shape=jax.ShapeDtypeStruct((T, N), F32),
        compiler_params=_params("parallel", "arbitrary"),
        name="outproj_residual",
    )(*parts, *([w] * n_parts), res)


def _silu(g):
    return g / (1.0 + jnp.exp(-g))


def _ffn_kernel(x_ref, g_ref, wg_ref, wu_ref, wd_ref, o_ref, xn_ref):
    @pl.when(pl.program_id(1) == 0)
    def _():
        x = x_ref[...]
        xn_ref[...] = (_rms(x) * g_ref[...]).astype(BF16)
        o_ref[...] = x

    xn = xn_ref[...]
    gate = jnp.dot(xn, wg_ref[...], preferred_element_type=F32)
    up = jnp.dot(xn, wu_ref[...], preferred_element_type=F32)
    hid = (_silu(gate) * up).astype(BF16)
    o_ref[...] += jnp.dot(hid, wd_ref[...], preferred_element_type=F32)


def _ffn_residual(x, g, w_gate, w_up, w_down, *, tm, tf):
    T, D = x.shape
    Fd = w_gate.shape[1]
    nf = Fd // tf
    return pl.pallas_call(
        _ffn_kernel,
        grid=(T // tm, nf),
        in_specs=[
            pl.BlockSpec((tm, D), lambda i, f: (i, 0)),
            pl.BlockSpec((1, D), lambda i, f: (0, 0)),
            pl.BlockSpec((D, tf), lambda i, f: (0, f)),
            pl.BlockSpec((D, tf), lambda i, f: (0, f)),
            pl.BlockSpec((tf, D), lambda i, f: (f, 0)),
        ],
        out_specs=pl.BlockSpec((tm, D), lambda i, f: (i, 0)),
        out_shape=jax.ShapeDtypeStruct((T, D), F32),
        scratch_shapes=[pltpu.VMEM((tm, D), BF16)],
        compiler_params=_params("parallel", "arbitrary"),
        name="ffn_residual",
    )(x, g.reshape(1, D), w_gate, w_up, w_down)


def _diff_kernel(slopes_ref, q_ref, k_ref, v_ref, dmat_ref, lq1_ref, lk1_ref, lq2_ref, lk2_ref,
                 subln_ref, o_ref, m_ref, l_ref, acc_ref, *, tq, tk, nk, lambda_init):
    h = pl.program_id(1)
    i = pl.program_id(2)
    j = pl.program_id(3)

    @pl.when(j == 0)
    def _():
        m_ref[...] = jnp.full(m_ref.shape, NEG_INF, F32)
        l_ref[...] = jnp.zeros(l_ref.shape, F32)
        acc_ref[...] = jnp.zeros(acc_ref.shape, F32)

    neg_slope = -slopes_ref[h] * LOG2E
    bias = _alibi_bias(dmat_ref[...], (j * tk - i * tq).astype(F32), neg_slope, tk)
    v = v_ref[...]
    for c in range(2):
        sl = slice(c * HEAD_DIM, (c + 1) * HEAD_DIM)
        s = lax.dot_general(q_ref[:, sl], k_ref[:, sl], _NT, preferred_element_type=F32) + bias
        m_prev = m_ref[c]
        m_new = jnp.maximum(m_prev, jnp.max(s, axis=1, keepdims=True))
        alpha = jnp.exp2(m_prev - m_new)
        p = jnp.exp2(s - m_new[:, :1])
        l_ref[c] = alpha * l_ref[c] + jnp.sum(p, axis=1, keepdims=True)
        acc_ref[c] = acc_ref[c] * alpha[:, :1] + jnp.dot(p.astype(BF16), v,
                                                         preferred_element_type=F32)
        m_ref[c] = m_new

    @pl.when(j == nk - 1)
    def _():
        _diff_finalize(acc_ref, l_ref, lq1_ref, lk1_ref, lq2_ref, lk2_ref, subln_ref, o_ref,
                       lambda_init)


def _alibi_bias(dbase, offset, neg_slope, tk):
    return jnp.concatenate([neg_slope * jnp.abs(dbase + (offset + float(cb * LANES)))
                            for cb in range(tk // LANES)], axis=1)


def _diff_finalize(acc_ref, l_ref, lq1_ref, lk1_ref, lq2_ref, lk2_ref, subln_ref, o_ref, lambda_init):
    lam = (jnp.exp(jnp.sum(lq1_ref[...] * lk1_ref[...], axis=-1, keepdims=True))
           - jnp.exp(jnp.sum(lq2_ref[...] * lk2_ref[...], axis=-1, keepdims=True))
           + lambda_init)
    o = acc_ref[0] / l_ref[0][:, :1] - lam * (acc_ref[1] / l_ref[1][:, :1])
    o = _rms(o) * subln_ref[...] * (1.0 - lambda_init)
    o_ref[...] = o.astype(o_ref.dtype)


def _diff_bounded_kernel(*refs, tq, tk, nk, lambda_init, n_cast):
    (slopes_ref, q_ref, k_ref, v_ref, dmat_ref, lq1_ref, lk1_ref, lq2_ref, lk2_ref,
     subln_ref) = refs[:10]
    cast_in, o_ref = refs[10:10 + n_cast], refs[10 + n_cast]
    cast_out = refs[11 + n_cast:11 + 2 * n_cast]
    l_ref, acc_ref = refs[-2:]
    for src, dst in zip(cast_in, cast_out):
        dst[...] = src[...].astype(dst.dtype)
    h = pl.program_id(1)
    i = pl.program_id(2)
    j = pl.program_id(3)

    @pl.when(j == 0)
    def _():
        l_ref[...] = jnp.zeros(l_ref.shape, F32)
        acc_ref[...] = jnp.zeros(acc_ref.shape, F32)

    neg_slope = -slopes_ref[h] * LOG2E
    q0 = i * tq
    k0 = j * tk
    v = v_ref[...]

    def scores(c):
        sl = slice(c * HEAD_DIM, (c + 1) * HEAD_DIM)
        return lax.dot_general(q_ref[:, sl], k_ref[:, sl], _NT, preferred_element_type=F32)

    overlaps = (k0 < q0 + tq) & (q0 < k0 + tk)

    @pl.when(overlaps)
    def _():
        for c in range(2):
            p = jnp.exp2(scores(c) + _alibi_bias(dmat_ref[...], (k0 - q0).astype(F32), neg_slope, tk))
            l_ref[c] += jnp.sum(p, axis=1, keepdims=True)
            acc_ref[c] += jnp.dot(p.astype(BF16), v, preferred_element_type=F32)

    @pl.when(jnp.logical_not(overlaps))
    def _():
        right = k0 >= q0 + tq
        gap = jnp.where(right, k0 - (q0 + tq), q0 - (k0 + tk)).astype(F32)
        key = lax.broadcasted_iota(jnp.int32, (1, tk), 1)
        qry = lax.broadcasted_iota(jnp.int32, (tq, 1), 0)
        key_bias = neg_slope * jnp.where(right, key, tk - key).astype(F32)
        qry_fac = jnp.exp2(neg_slope * (jnp.where(right, tq - qry, qry).astype(F32) + gap))
        for c in range(2):
            p = jnp.exp2(scores(c) + key_bias)
            l_ref[c] += qry_fac * jnp.sum(p, axis=1, keepdims=True)
            acc_ref[c] += qry_fac * jnp.dot(p.astype(BF16), v, preferred_element_type=F32)

    @pl.when(j == nk - 1)
    def _():
        _diff_finalize(acc_ref, l_ref, lq1_ref, lk1_ref, lq2_ref, lk2_ref, subln_ref, o_ref,
                       lambda_init)


BF16_SUBLANES = 16


def _cast_rows_per_step(n_rows, n_steps):
    for share in (1, 2, 4, 8):
        rows, rem = divmod(n_rows * share, n_steps)
        if rem == 0 and rows % BF16_SUBLANES == 0 and n_steps % share == 0:
            return rows, share
    return None


def _diff_attention(proj, B, S, slopes, lam_vecs, subln, lambda_init, *, tq, tk, bounded,
                    cast_2d=()):
    H = C_HEADS
    hw = 2 * HEAD_DIM
    nq, nk = S // tq, S // tk
    dmat = (jnp.arange(LANES, dtype=F32)[None, :] - jnp.arange(tq, dtype=F32)[:, None])
    stat = [pltpu.VMEM((2, tq, LANES), F32)]
    kw = dict(tq=tq, tk=tk, nk=nk, lambda_init=lambda_init)
    cast_specs = []
    if bounded:
        body, name = _diff_bounded_kernel, "diff_bounded"
        kw["n_cast"] = len(cast_2d)
        for a in cast_2d:
            rows, share = _cast_rows_per_step(a.shape[0], B * H * nq * nk)
            cast_specs.append(pl.BlockSpec(
                (rows, a.shape[1]),
                lambda b, h, i, j, s, share=share: ((((b * H + h) * nq + i) * nk + j) // share, 0)))
    else:
        assert not cast_2d
        body, name, stat = _diff_kernel, "diff_attention", stat * 2
    o_spec = pl.BlockSpec((tq, hw), lambda b, h, i, j, s: (b * nq + i, h))
    o_shape = jax.ShapeDtypeStruct((B * S, H * hw), BF16)
    vec_spec = pl.BlockSpec((1, HEAD_DIM), lambda b, h, i, j, s: (0, 0))
    grid_spec = pltpu.PrefetchScalarGridSpec(
        num_scalar_prefetch=1,
        grid=(B, H, nq, nk),
        in_specs=[
            pl.BlockSpec((tq, hw), lambda b, h, i, j, s: (b * nq + i, h)),
            pl.BlockSpec((tk, hw), lambda b, h, i, j, s: (b * nk + j, H + h)),
            pl.BlockSpec((tk, hw), lambda b, h, i, j, s: (b * nk + j, 2 * H + h)),
            pl.BlockSpec((tq, LANES), lambda b, h, i, j, s: (0, 0)),
            vec_spec, vec_spec, vec_spec, vec_spec,
            pl.BlockSpec((1, hw), lambda b, h, i, j, s: (0, 0)),
        ] + cast_specs,
        out_specs=[o_spec] + cast_specs if cast_specs else o_spec,
        scratch_shapes=stat + [pltpu.VMEM((2, tq, hw), F32)],
    )
    cast_shapes = [jax.ShapeDtypeStruct(a.shape, BF16) for a in cast_2d]
    return pl.pallas_call(
        functools.partial(body, **kw),
        grid_spec=grid_spec,
        out_shape=[o_shape] + cast_shapes if cast_specs else o_shape,
        compiler_params=_params("arbitrary", "arbitrary", "arbitrary", "arbitrary"),
        name=name,
    )(slopes, proj, proj, proj, dmat, *[v.reshape(1, HEAD_DIM) for v in lam_vecs],
      subln.reshape(1, hw), *cast_2d)


def _router_kernel(x_ref, g_ref, wr_ref, meta_ref, cnt_ref, base_ref, *, tm, n_exp):
    i = pl.program_id(0)

    @pl.when(i == 0)
    def _():
        base_ref[...] = jnp.zeros(base_ref.shape, F32)

    hn = _rms(x_ref[...]) * g_ref[...]
    logits = jnp.dot(hn, wr_ref[...], preferred_element_type=F32,
                     precision=lax.Precision.HIGHEST)
    lane = lax.broadcasted_iota(jnp.int32, (tm, LANES), 1)
    logits = jnp.where(lane < n_exp, logits, NEG_INF)
    t1 = jnp.max(logits, axis=1, keepdims=True)
    i1 = jnp.min(jnp.where(logits == t1, lane, LANES), axis=1, keepdims=True)
    rest = jnp.where(lane == i1, NEG_INF, logits)
    t2 = jnp.max(rest, axis=1, keepdims=True)
    i2 = jnp.min(jnp.where(rest == t2, lane, LANES), axis=1, keepdims=True)
    ex = jnp.exp(t2 - t1)
    g1 = 1.0 / (1.0 + ex)
    g2 = ex * g1
    oh1 = lane == i1
    oh2 = lane == i2
    member = jnp.where(oh1 | oh2, 1.0, 0.0)
    row = lax.broadcasted_iota(jnp.int32, (tm, tm), 0)
    col = lax.broadcasted_iota(jnp.int32, (tm, tm), 1)
    strict_lower = jnp.where(col < row, 1.0, 0.0).astype(BF16)
    before = jnp.dot(strict_lower, member.astype(BF16), preferred_element_type=F32) + base_ref[...]
    r1 = jnp.sum(jnp.where(oh1, before, 0.0), axis=1, keepdims=True)
    r2 = jnp.sum(jnp.where(oh2, before, 0.0), axis=1, keepdims=True)
    base_ref[...] += jnp.sum(member, axis=0, keepdims=True)
    meta = jnp.where(lane == 0, i1.astype(F32), 0.0)
    meta = jnp.where(lane == 1, i2.astype(F32), meta)
    meta = jnp.where(lane == 2, g1, meta)
    meta = jnp.where(lane == 3, g2, meta)
    meta = jnp.where(lane == 4, r1, meta)
    meta = jnp.where(lane == 5, r2, meta)
    meta_ref[...] = meta
    cnt_ref[...] = base_ref[...]


def _router(x, g, w_router, *, tm):
    T, D = x.shape
    n_exp = w_router.shape[1]
    wr = jnp.zeros((D, LANES), F32).at[:, :n_exp].set(w_router)
    return pl.pallas_call(
        functools.partial(_router_kernel, tm=tm, n_exp=n_exp),
        grid=(T // tm,),
        in_specs=[
            pl.BlockSpec((tm, D), lambda i: (i, 0)),
            pl.BlockSpec((1, D), lambda i: (0, 0)),
            pl.BlockSpec((D, LANES), lambda i: (0, 0)),
        ],
        out_specs=[pl.BlockSpec((tm, LANES), lambda i: (i, 0)),
                   pl.BlockSpec((1, LANES), lambda i: (0, 0))],
        out_shape=[jax.ShapeDtypeStruct((T, LANES), F32),
                   jax.ShapeDtypeStruct((1, LANES), F32)],
        scratch_shapes=[pltpu.VMEM((1, LANES), F32)],
        compiler_params=_params("arbitrary"),
        name="moe_router",
    )(x, g.reshape(1, D), wr)


ISSUE_UNROLL = 8


def _row_copy(src_ref, src_row, dst_ref, dst_row, sem):
    return pltpu.make_async_copy(src_ref.at[pl.ds(src_row, 1)], dst_ref.at[pl.ds(dst_row, 1)], sem)


def _dispatch_kernel(pend_ref, dest_hbm, x_ref, xs_hbm, idx_ref, zero_ref, idx_sem, row_sem, zero_sem,
                     *, tm, bm, n_rows):
    i = pl.program_id(0)

    @pl.when(i == 0)
    def _():
        zero_ref[...] = jnp.zeros(zero_ref.shape, F32)
        fills = [pltpu.make_async_copy(zero_ref, xs_hbm.at[pl.ds(n_rows - (e + 1) * bm, bm)], zero_sem)
                 for e in range(N_EXPERTS)]
        for e in range(N_EXPERTS):
            start = pl.multiple_of(jnp.maximum(pend_ref[e] - bm, 0), 8)
            fills.append(pltpu.make_async_copy(zero_ref, xs_hbm.at[pl.ds(start, bm)], zero_sem))
        for f in fills:
            f.start()
            f.wait()

    n_idx = TOP_K * tm
    idx_copy = pltpu.make_async_copy(dest_hbm.at[pl.ds(i * n_idx, n_idx)], idx_ref, idx_sem)
    idx_copy.start()
    idx_copy.wait()

    def issue(t, carry):
        for k in range(TOP_K):
            _row_copy(x_ref, t, xs_hbm, idx_ref[TOP_K * t + k], row_sem).start()
        return carry

    lax.fori_loop(0, tm, issue, 0, unroll=ISSUE_UNROLL)

    for k in range(TOP_K):
        pltpu.make_async_copy(x_ref, xs_hbm.at[pl.ds(0, tm)], row_sem).wait()


def _dispatch(dest, x, pend, n_rows, *, tm, bm):
    T, D = x.shape
    grid_spec = pltpu.PrefetchScalarGridSpec(
        num_scalar_prefetch=1,
        grid=(T // tm,),
        in_specs=[pl.BlockSpec(memory_space=pl.ANY),
                  pl.BlockSpec((tm, D), lambda i, pend: (i, 0))],
        out_specs=pl.BlockSpec(memory_space=pl.ANY),
        scratch_shapes=[pltpu.SMEM((TOP_K * tm,), jnp.int32), pltpu.VMEM((bm, D), F32),
                        pltpu.SemaphoreType.DMA(()), pltpu.SemaphoreType.DMA(()),
                        pltpu.SemaphoreType.DMA(())],
    )
    return pl.pallas_call(
        functools.partial(_dispatch_kernel, tm=tm, bm=bm, n_rows=n_rows),
        grid_spec=grid_spec,
        out_shape=jax.ShapeDtypeStruct((n_rows, D), F32),
        compiler_params=pltpu.CompilerParams(dimension_semantics=("arbitrary",),
                                             vmem_limit_bytes=VMEM_LIMIT_BYTES,
                                             has_side_effects=True),
        name="moe_dispatch",
    )(pend.astype(jnp.int32), dest, x)


def _expert_kernel(be_ref, bn_ref, x_ref, g_ref, wg_ref, wu_ref, wd_ref, y_ref, xn_ref, *, bm):
    del be_ref
    n_valid = bn_ref[pl.program_id(0)]

    @pl.when(pl.program_id(1) == 0)
    def _():
        rows = lax.broadcasted_iota(jnp.int32, (bm, 1), 0)
        x = jnp.where(rows < n_valid, x_ref[...], 0.0)
        xn_ref[...] = (_rms(x) * g_ref[...]).astype(BF16)
        y_ref[...] = jnp.zeros(y_ref.shape, F32)

    def ffn_rows(rs):
        xn = xn_ref[rs, :]
        gate = jnp.dot(xn, wg_ref[...], preferred_element_type=F32)
        up = jnp.dot(xn, wu_ref[...], preferred_element_type=F32)
        hid = (_silu(gate) * up).astype(BF16)
        y_ref[rs, :] += jnp.dot(hid, wd_ref[...], preferred_element_type=F32)

    half = bm // 2

    @pl.when(n_valid > half)
    def _():
        ffn_rows(slice(0, bm))

    @pl.when((n_valid > 0) & (n_valid <= half))
    def _():
        ffn_rows(slice(0, half))


def _expert_ffn(xs, g, e_gate, e_up, e_down, blk_expert, blk_valid, *, bm, tf):
    P, D = xs.shape
    Fd = e_gate.shape[2]
    nf = Fd // tf
    nblk = P // bm

    def f_idx(b, f, bn):
        return jnp.where(bn[b] > 0, f, nf - 1)

    grid_spec = pltpu.PrefetchScalarGridSpec(
        num_scalar_prefetch=2,
        grid=(nblk, nf),
        in_specs=[
            pl.BlockSpec((bm, D), lambda b, f, be, bn: (b, 0)),
            pl.BlockSpec((1, D), lambda b, f, be, bn: (0, 0)),
            pl.BlockSpec((None, D, tf), lambda b, f, be, bn: (be[b], 0, f_idx(b, f, bn))),
            pl.BlockSpec((None, D, tf), lambda b, f, be, bn: (be[b], 0, f_idx(b, f, bn))),
            pl.BlockSpec((None, tf, D), lambda b, f, be, bn: (be[b], f_idx(b, f, bn), 0)),
        ],
        out_specs=pl.BlockSpec((bm, D), lambda b, f, be, bn: (b, 0)),
        scratch_shapes=[pltpu.VMEM((bm, D), BF16)],
    )
    return pl.pallas_call(
        functools.partial(_expert_kernel, bm=bm),
        grid_spec=grid_spec,
        out_shape=jax.ShapeDtypeStruct((P, D), F32),
        compiler_params=_params("arbitrary", "arbitrary"),
        name="moe_experts",
    )(blk_expert, blk_valid, xs, g.reshape(1, D), e_gate, e_up, e_down)


def _combine_kernel(dest_hbm, ys_hbm, x_ref, meta_ref, o_ref, idx_ref, ybuf_ref, idx_sem, row_sem,
                    *, tm):
    i = pl.program_id(0)
    n_idx = TOP_K * tm
    idx_copy = pltpu.make_async_copy(dest_hbm.at[pl.ds(i * n_idx, n_idx)], idx_ref, idx_sem)
    idx_copy.start()
    idx_copy.wait()

    def issue(t, carry):
        for k in range(TOP_K):
            _row_copy(ys_hbm, idx_ref[TOP_K * t + k], ybuf_ref.at[k], t, row_sem).start()
        return carry

    lax.fori_loop(0, tm, issue, 0, unroll=ISSUE_UNROLL)

    for k in range(TOP_K):
        pltpu.make_async_copy(ys_hbm.at[pl.ds(0, tm)], ybuf_ref.at[k], row_sem).wait()

    meta = meta_ref[...]
    out = x_ref[...]
    for k in range(TOP_K):
        out = out + meta[:, 2 + k:3 + k] * ybuf_ref[k]
    o_ref[...] = out


def _combine(dest, ys, x, meta, *, tm):
    T, D = x.shape
    return pl.pallas_call(
        functools.partial(_combine_kernel, tm=tm),
        grid=(T // tm,),
        in_specs=[pl.BlockSpec(memory_space=pl.ANY), pl.BlockSpec(memory_space=pl.ANY),
                  pl.BlockSpec((tm, D), lambda i: (i, 0)),
                  pl.BlockSpec((tm, LANES), lambda i: (i, 0))],
        out_specs=pl.BlockSpec((tm, D), lambda i: (i, 0)),
        out_shape=jax.ShapeDtypeStruct((T, D), F32),
        scratch_shapes=[pltpu.SMEM((TOP_K * tm,), jnp.int32),
                        pltpu.VMEM((TOP_K, tm, D), F32),
                        pltpu.SemaphoreType.DMA(()), pltpu.SemaphoreType.DMA(())],
        compiler_params=_params("arbitrary"),
        name="moe_combine",
    )(dest, ys, x, meta)


def _moe_residual(x, g, w_router, e_gate, e_up, e_down, *, tm_route, tm_move, bm, tf):
    T, D = x.shape
    meta, counts = _router(x, g, w_router, tm=tm_route)
    counts = counts[0, :N_EXPERTS].astype(jnp.int32)
    padded = (counts + bm - 1) // bm * bm
    pend = jnp.cumsum(padded)
    pstart = pend - padded
    expert = meta[:, 0:TOP_K].astype(jnp.int32)
    rank = meta[:, 4:4 + TOP_K].astype(jnp.int32)
    dest = (pstart[expert] + rank).reshape(T * TOP_K)
    n_rows = T * TOP_K + N_EXPERTS * bm
    nblk = n_rows // bm
    blk_row0 = jnp.arange(nblk, dtype=jnp.int32) * bm
    blk_expert = jnp.minimum(jnp.searchsorted(pend, blk_row0, side='right'),
                             N_EXPERTS - 1).astype(jnp.int32)
    blk_valid = jnp.clip(counts[blk_expert] - (blk_row0 - pstart[blk_expert]), 0, bm)
    blk_valid = jnp.where(blk_row0 < pend[-1], blk_valid, 0).astype(jnp.int32)
    last_used = jnp.maximum(pend[-1] // bm - 1, 0)
    blk_expert = jnp.where(blk_row0 < pend[-1], blk_expert, blk_expert[last_used])

    xs = _dispatch(dest, x, pend, n_rows, tm=tm_move, bm=bm)
    ys = _expert_ffn(xs, g, e_gate, e_up, e_down, blk_expert, blk_valid, bm=bm, tf=tf)
    return _combine(dest, ys, x, meta, tm=tm_move)


def _alibi_slopes(n_heads):
    return 2.0 ** (-8.0 * jnp.arange(1, n_heads + 1, dtype=F32) / n_heads)


def _tile(n, want):
    t = min(n, want)
    while n % t:
        t //= 2
    return t


Q_SCALE = HEAD_DIM ** -0.5 * LOG2E
PROJ_TN = 512
BOUNDED_SCORE_LIMIT = 60.0


class _Tiles(NamedTuple):
    rows: int
    out_cols: int
    attn_q: int
    attn_k: int
    diff_k: int
    online_q: int
    online_k: int
    band_q: int
    ffn_rows: int
    ffn_hidden: int
    moe_rows: int
    moe_hidden: int


def _tiles(S, T, d_ff_expert=1024):
    return _Tiles(rows=_tile(S, 1024), out_cols=1024,
                  attn_q=_tile(S, 1024), attn_k=_tile(S, 4096), diff_k=_tile(S, 4096),
                  online_q=_tile(S, 512), online_k=_tile(S, 1024), band_q=_tile(S, 512),
                  ffn_rows=_tile(T, 512), ffn_hidden=512,
                  moe_rows=_tile(T, 512), moe_hidden=_tile(d_ff_expert, 1024))


def _score_bound(q_gain, k_gain):
    return 1.01 * HEAD_DIM * Q_SCALE * jnp.max(jnp.abs(q_gain)) * jnp.max(jnp.abs(k_gain))


def _layer0(x, norm_mix, w_in, qn_a, kn_a, qn_b, kn_b, w_out, norm_ffn, w_gate, w_up, w_down):
    B, S, D = x.shape
    T = B * S
    xf = x.reshape(T, D)
    t = _tiles(S, T)
    a_q, a_kv, b_dim = A_HEADS * HEAD_DIM, A_KV_HEADS * HEAD_DIM, B_HEADS * HEAD_DIM
    w_in = w_in.astype(BF16)
    ops_a = ([(0, True, Q_SCALE)] * A_HEADS + [(1, True, 1.0)] * A_KV_HEADS
             + [(None, False, 1.0)] * A_KV_HEADS)
    ops_b = [(0, False, Q_SCALE)] * B_HEADS + [(1, False, 1.0)] * B_HEADS + [(None, False, 1.0)] * B_HEADS
    proj_a = _normproj(xf, norm_mix, w_in, ops_a, [qn_a, kn_a], _rope_tables(S), col0=0,
                       tm=t.rows, tn=PROJ_TN)
    proj_b = _normproj(xf, norm_mix, w_in, ops_b, [qn_b, kn_b], None, col0=a_q + 2 * a_kv,
                       tm=t.rows, tn=PROJ_TN)
    gqa = functools.partial(_gqa_attention, B=B, S=S, q_col0=0, k_col0=a_q, v_col0=a_q + a_kv)
    o_a = lax.cond(_score_bound(qn_a, kn_a) <= BOUNDED_SCORE_LIMIT,
                   lambda p: gqa(p, tq=t.attn_q, tk=t.attn_k, bounded=True),
                   lambda p: gqa(p, tq=t.online_q, tk=t.online_k, bounded=False), proj_a)
    dil = functools.partial(_dilated_attention, B=B, S=S, slopes=_alibi_slopes(B_HEADS), q_col0=0,
                            k_col0=b_dim, v_col0=2 * b_dim, tq=t.band_q)
    o_b = lax.cond(_score_bound(qn_b, kn_b) <= BOUNDED_SCORE_LIMIT,
                   lambda p: dil(p, bounded=True), lambda p: dil(p, bounded=False), proj_b)
    x1 = _outproj_residual([o_a, o_b], w_out.astype(BF16), xf, tm=t.rows, tn=t.out_cols)
    x2 = _ffn_residual(x1, norm_ffn, w_gate.astype(BF16), w_up.astype(BF16), w_down.astype(BF16),
                       tm=t.ffn_rows, tf=t.ffn_hidden)
    return x2.reshape(B, S, D)


def _layer1(x, norm_mix, w_in, qn_c, kn_c, lam_q1, lam_k1, lam_q2, lam_k2, subln, w_out, norm_ffn,
            w_router, e_gate, e_up, e_down):
    B, S, D = x.shape
    T = B * S
    xf = x.reshape(T, D)
    t = _tiles(S, T, e_gate.shape[2])
    head_ops = ([(0, False, Q_SCALE)] * (2 * C_HEADS) + [(1, False, 1.0)] * (2 * C_HEADS)
                + [(None, False, 1.0)] * (2 * C_HEADS))
    proj = _normproj(xf, norm_mix, w_in.astype(BF16), head_ops, [qn_c, kn_c], None, col0=0,
                     tm=t.rows, tn=2 * PROJ_TN)
    lambda_init = 0.8 - 0.6 * math.exp(-0.3 * 1)
    diff = functools.partial(_diff_attention, B=B, S=S, slopes=_alibi_slopes(C_HEADS),
                             lam_vecs=(lam_q1, lam_k1, lam_q2, lam_k2), subln=subln,
                             lambda_init=lambda_init)
    tq, tk = t.attn_q, t.diff_k
    n_steps = B * C_HEADS * (S // tq) * (S // tk)
    rows_of = lambda w: math.prod(w.shape[:-1])
    stream_casts = all(_cast_rows_per_step(rows_of(w), n_steps) for w in (e_gate, e_up, e_down))

    def bounded_branch(p, *ws):
        if not stream_casts:
            return (diff(p, tq=tq, tk=tk, bounded=True), *[w.astype(BF16) for w in ws])
        o, *cast = diff(p, tq=tq, tk=tk, bounded=True,
                        cast_2d=[w.reshape(-1, w.shape[-1]) for w in ws])
        return (o, *[c.reshape(w.shape) for c, w in zip(cast, ws)])

    def online_branch(p, *ws):
        return (diff(p, tq=t.online_q, tk=t.online_k, bounded=False),
                *[w.astype(BF16) for w in ws])

    o_c, eg, eu, ed = lax.cond(_score_bound(qn_c, kn_c) <= BOUNDED_SCORE_LIMIT,
                               bounded_branch, online_branch, proj, e_gate, e_up, e_down)
    x3 = _outproj_residual([o_c], w_out.astype(BF16), xf, tm=t.rows, tn=t.out_cols)
    out = _moe_residual(x3, norm_ffn, w_router, eg, eu, ed, tm_route=t.moe_rows,
                        tm_move=t.moe_rows, bm=t.moe_rows, tf=t.moe_hidden)
    return out.reshape(B, S, D)


def kernel(x, l0_norm_mix,l0_w_in, l0_qnorm_a, l0_knorm_a, l0_qnorm_b, l0_knorm_b, l0_w_out, l0_norm_ffn, l0_w_gate, l0_w_up, l0_w_down, l1_norm_mix, l1_w_in, l1_qnorm_c, l1_knorm_c, l1_lambda_q1, l1_lambda_k1, l1_lambda_q2, l1_lambda_k2, l1_subln, l1_w_out, l1_norm_ffn, l1_w_router, l1_e_gate, l1_e_up, l1_e_down):
    x = _layer0(x, l0_norm_mix, l0_w_in, l0_qnorm_a, l0_knorm_a, l0_qnorm_b, l0_knorm_b, l0_w_out,
                l0_norm_ffn, l0_w_gate, l0_w_up, l0_w_down)
    return _layer1(x, l1_norm_mix, l1_w_in, l1_qnorm_c, l1_knorm_c, l1_lambda_q1, l1_lambda_k1,
                   l1_lambda_q2, l1_lambda_k2, l1_subln, l1_w_out, l1_norm_ffn, l1_w_router,
                   l1_e_gate, l1_e_up, l1_e_down)
```
